```python
import math
import jax, jax.numpy as jnp
from jax import lax
import numpy as np

D_MODEL = 1024
BATCH = 4
SEQ = 8192
DEPTH = 4

GRID_W = 64
CTX_LEN = 256
N_MIXERS = 3
NORM_EPS = 1e-6
RW_HEAD = 64
RW_HEADS = D_MODEL // RW_HEAD
RW_DECAY_LORA = 64
RW_AAA_LORA = 64
RW_GATE_LORA = 160
RW_GN_EPS = 64e-5
MLA_HEADS = 8
MLA_Q_LORA = 384
MLA_KV_LORA = 256
MLA_NOPE = 128
MLA_ROPE = 64
MLA_V = 128
MLA_QK = MLA_NOPE + MLA_ROPE
ATTN_BLOCK = 128
ROPE_THETA = 10000.0
GDN_K_HEADS = 8
GDN_V_HEADS = 16
GDN_HEAD_K = 128
GDN_HEAD_V = 128
GDN_CONV = 5
GDN_CHUNK = 64
GDN_QK_DIM = GDN_K_HEADS * GDN_HEAD_K
GDN_V_DIM = GDN_V_HEADS * GDN_HEAD_V
GDN_IN = 2 * GDN_QK_DIM + 2 * GDN_V_DIM + 4 * GDN_V_HEADS
N_EXPERTS = 16
EC_FACTOR = 2
D_EXPERT = 2816

kernel_name = 'hybrid_rwkv7_mla_gdn_ecmoe_dit'


def rms_norm(x, g, eps=NORM_EPS):
    xf = x.astype(jnp.float32)
    y = xf * lax.rsqrt(jnp.mean(xf * xf, -1, keepdims=True) + eps)
    return (y * g.astype(jnp.float32)).astype(x.dtype)


def l2_normalize(x, eps=1e-6):
    xf = x.astype(jnp.float32)
    return (xf * lax.rsqrt(jnp.sum(xf * xf, -1, keepdims=True) + eps)).astype(x.dtype)


def centred_shift(h):
    prev = jnp.pad(h[:, :-1], ((0, 0), (1, 0), (0, 0)))
    nxt = jnp.pad(h[:, 1:], ((0, 0), (0, 1), (0, 0)))
    return 0.5 * (prev + nxt) - h


def centred_depthwise_conv(u, w):
    ch = u.shape[-1]
    return lax.conv_general_dilated(u, w[:, None, :].astype(u.dtype), window_strides=(1,),
                                    padding=[(GDN_CONV // 2, GDN_CONV // 2)],
                                    dimension_numbers=('NWC', 'WIO', 'NWC'), feature_group_count=ch)


def axial_rope(t, n):
    rows = n // GRID_W
    row_id = jnp.repeat(jnp.arange(rows), GRID_W)
    col_id = jnp.tile(jnp.arange(GRID_W), rows)
    nf = MLA_ROPE // 4
    inv = ROPE_THETA ** (-jnp.arange(nf, dtype=jnp.float32) / nf)

    def rot(seg, pos):
        ang = pos.astype(jnp.float32)[:, None] * inv
        cos = jnp.cos(ang)[:, None, :].astype(t.dtype)
        sin = jnp.sin(ang)[:, None, :].astype(t.dtype)
        s1, s2 = seg[..., :nf], seg[..., nf:]
        return jnp.concatenate([s1 * cos - s2 * sin, s2 * cos + s1 * sin], -1)

    half = MLA_ROPE // 2
    return jnp.concatenate([rot(t[..., :half], row_id), rot(t[..., half:], col_id)], -1)


def rwkv_scan(S0, r, w, k, v, a, b, reverse):
    def step(S, inp):
        r_t, w_t, k_t, v_t, a_t, b_t = inp
        sa = jnp.einsum('bhij,bhj->bhi', S, a_t)
        S = S * w_t[:, :, None, :] + sa[..., None] * b_t[:, :, None, :] + v_t[..., None] * k_t[:, :, None, :]
        return S, jnp.einsum('bhij,bhj->bhi', S, r_t)

    xs = tuple(jnp.moveaxis(t, 1, 0) for t in (r, w, k, v, a, b))
    S, o = lax.scan(step, S0, xs, reverse=reverse)
    return S, jnp.moveaxis(o, 0, 1)


def rwkv_mixer(h_ctx, h_lat, mu, w_rkv, w_o, w0, w1, w2, a0, a1, a2, g1, g2, vecs, need_ctx):
    f32 = jnp.float32
    k_k, k_a, r_k, ln_w, ln_b = (vecs[j] for j in range(5))

    def heads(t):
        return t.reshape(t.shape[:-1] + (RW_HEADS, RW_HEAD))

    def prep(h):
        xx = centred_shift(h)
        xr, xw, xk, xv, xa, xg = (h + xx * mu[j] for j in range(6))
        r = heads(xr @ w_rkv[0])
        k = xk @ w_rkv[1]
        v = heads(xv @ w_rkv[2])
        gate = jax.nn.sigmoid(xg @ g1) @ g2
        kk = l2_normalize(heads(k * k_k))
        per_dir = []
        for d in range(2):
            w_log = -jax.nn.softplus(-(w0[d] + jnp.tanh(xw @ w1[d]) @ w2[d])) - 0.5
            decay = jnp.exp(-jnp.exp(heads(w_log).astype(f32)))
            a = jax.nn.sigmoid(a0[d] + (xa @ a1[d]) @ a2[d])
            k_d = heads(k * (1.0 + (a - 1.0) * k_a))
            per_dir.append((decay, k_d, heads(a)))
        return r, v, kk, gate, per_dir

    def scans(r, v, kk, per_dir, S_f, S_b):
        (dec_f, k_f, a_f), (dec_b, k_b, a_b) = per_dir
        rf, vf, kkf = r.astype(f32), v.astype(f32), kk.astype(f32)
        Sf, o_f = rwkv_scan(S_f, rf, dec_f, k_f.astype(f32), vf, -kkf, kkf * a_f.astype(f32), reverse=False)
        Sb, o_b = rwkv_scan(S_b, rf, dec_b, k_b.astype(f32), vf, -kkf, kkf * a_b.astype(f32), reverse=True)
        bonus = jnp.sum(r * (k_f + k_b) * heads(r_k), -1, keepdims=True) * v
        return Sf, Sb, o_f + o_b, bonus

    def finish(o_sum, bonus, gate, dtype):
        mean = jnp.mean(o_sum, -1, keepdims=True)
        var = jnp.mean(jnp.square(o_sum - mean), -1, keepdims=True)
        y = ((o_sum - mean) * lax.rsqrt(var + RW_GN_EPS)).astype(dtype)
        y = y * heads(ln_w) + heads(ln_b) + bonus
        y = y.reshape(y.shape[:-2] + (D_MODEL,)) * gate
        return y @ w_o

    B = h_lat.shape[0]
    S0 = jnp.zeros((B, RW_HEADS, RW_HEAD, RW_HEAD), f32)
    rc, vc, kkc, gc, dc = prep(h_ctx)
    Sf, Sb, oc, bc = scans(rc, vc, kkc, dc, S0, S0)
    rl, vl, kkl, gl, dl = prep(h_lat)
    _, _, ol, bl = scans(rl, vl, kkl, dl, Sf, Sb)
    y_ctx = finish(oc, bc, gc, h_ctx.dtype) if need_ctx else None
    return y_ctx, finish(ol, bl, gl, h_lat.dtype)


def mla_project(h, w_in, qa_g, kva_g, w_qb, w_kvb, qk_g):
    B, T, _ = h.shape
    u = h @ w_in
    cq = rms_norm(u[..., :MLA_Q_LORA], qa_g)
    ckv = rms_norm(u[..., MLA_Q_LORA:MLA_Q_LORA + MLA_KV_LORA], kva_g)
    k_pe = u[..., MLA_Q_LORA + MLA_KV_LORA:]
    q = (cq @ w_qb).reshape(B, T, MLA_HEADS, MLA_QK)
    kv = (ckv @ w_kvb).reshape(B, T, MLA_HEADS, MLA_NOPE + MLA_V)
    k_nope, v = kv[..., :MLA_NOPE], kv[..., MLA_NOPE:]
    k = jnp.concatenate([k_nope, jnp.broadcast_to(k_pe[:, :, None, :], (B, T, MLA_HEADS, MLA_ROPE))], -1)
    return rms_norm(q, qk_g[0]), rms_norm(k, qk_g[1]), v


def mla_attend(q, k, v):
    s = jnp.einsum('bqhd,bkhd->bhqk', q, k).astype(jnp.float32) * (MLA_QK ** -0.5)
    p = jax.nn.softmax(s, axis=-1).astype(v.dtype)
    return jnp.einsum('bhqk,bkhv->bqhv', p, v)


def mla_mixer(h_ctx, h_lat, w_in, qa_g, kva_g, w_qb, w_kvb, qk_g, w_o, need_ctx):
    B, n, _ = h_lat.shape
    qc, kc, vc = mla_project(h_ctx, w_in, qa_g, kva_g, w_qb, w_kvb, qk_g)
    ql, kl, vl = mla_project(h_lat, w_in, qa_g, kva_g, w_qb, w_kvb, qk_g)
    ql = jnp.concatenate([ql[..., :MLA_NOPE], axial_rope(ql[..., MLA_NOPE:], n)], -1)
    kl = jnp.concatenate([kl[..., :MLA_NOPE], axial_rope(kl[..., MLA_NOPE:], n)], -1)
    k_all = jnp.concatenate([kl, kc], 1)
    v_all = jnp.concatenate([vl, vc], 1)
    nb = n // ATTN_BLOCK
    qb = jnp.moveaxis(ql.reshape(B, nb, ATTN_BLOCK, MLA_HEADS, MLA_QK), 1, 0)
    ob = lax.map(lambda qblk: mla_attend(qblk, k_all, v_all), qb)
    o_lat = jnp.moveaxis(ob, 0, 1).reshape(B, n, MLA_HEADS * MLA_V) @ w_o
    y_ctx = None
    if need_ctx:
        y_ctx = mla_attend(qc, kc, vc).reshape(B, h_ctx.shape[1], MLA_HEADS * MLA_V) @ w_o
    return y_ctx, o_lat


def gated_delta_chunked(q, k, v, g, beta, S0):
    B, T, H, K = q.shape
    V = v.shape[-1]
    C = GDN_CHUNK
    n = T // C

    def chunks(t):
        return jnp.moveaxis(t.reshape((B, n, C, H) + t.shape[3:]), 3, 1)

    q = chunks(q * (K ** -0.5))
    k = chunks(k)
    v = chunks(v)
    beta = chunks(beta)
    gc = jnp.cumsum(chunks(g), axis=-1)
    kb = k * beta[..., None]
    vb = v * beta[..., None]
    idx = jnp.arange(C)
    lower = idx[:, None] >= idx[None, :]
    strict = idx[:, None] > idx[None, :]
    decay = jnp.exp(jnp.where(lower, gc[..., :, None] - gc[..., None, :], -jnp.inf))
    L = jnp.where(strict, jnp.einsum('bhnik,bhnjk->bhnij', kb, k) * decay, 0.0)
    A = L + jnp.eye(C, dtype=L.dtype)
    u = lax.linalg.triangular_solve(A, vb, left_side=True, lower=True, unit_diagonal=True)
    w = lax.linalg.triangular_solve(A, kb * jnp.exp(gc)[..., None], left_side=True, lower=True, unit_diagonal=True)
    qk = jnp.einsum('bhnik,bhnjk->bhnij', q, k) * decay

    def step(S, inp):
        q_i, k_i, u_i, w_i, g_i, qk_i = inp
        v_new = u_i - jnp.einsum('bhck,bhkv->bhcv', w_i, S)
        o = jnp.einsum('bhck,bhkv->bhcv', q_i * jnp.exp(g_i)[..., None], S) + jnp.einsum('bhij,bhjv->bhiv', qk_i, v_new)
        g_last = g_i[..., -1:]
        S = S * jnp.exp(g_last)[..., None] + jnp.einsum('bhck,bhcv->bhkv', k_i * jnp.exp(g_last - g_i)[..., None], v_new)
        return S, o

    xs = tuple(jnp.moveaxis(t, 2, 0) for t in (q, k, u, w, gc, qk))
    S, o = lax.scan(step, S0, xs)
    o = jnp.moveaxis(jnp.moveaxis(o, 0, 2), 1, 3).reshape(B, T, H, V)
    return S, o


def gdn_mixer(h_ctx, h_lat, w_in, conv_w, a_log, dt_bias, norm_g, w_out, need_ctx):
    f32 = jnp.float32
    qk_end = 2 * GDN_QK_DIM
    v_end = qk_end + GDN_V_DIM
    z_end = v_end + GDN_V_DIM
    b_end = z_end + 2 * GDN_V_HEADS
    rep = GDN_V_HEADS // GDN_K_HEADS

    def prep(h):
        B, T, _ = h.shape
        u = h @ w_in
        qkv = jax.nn.silu(centred_depthwise_conv(u[..., :v_end], conv_w))
        q = l2_normalize(qkv[..., :GDN_QK_DIM].reshape(B, T, GDN_K_HEADS, GDN_HEAD_K))
        k = l2_normalize(qkv[..., GDN_QK_DIM:qk_end].reshape(B, T, GDN_K_HEADS, GDN_HEAD_K))
        v = qkv[..., qk_end:].reshape(B, T, GDN_V_HEADS, GDN_HEAD_V).astype(f32)
        q = jnp.repeat(q, rep, axis=2).astype(f32)
        k = jnp.repeat(k, rep, axis=2).astype(f32)
        z = u[..., v_end:z_end].reshape(B, T, GDN_V_HEADS, GDN_HEAD_V)
        beta = jax.nn.sigmoid(u[..., z_end:b_end].astype(f32)).reshape(B, T, 2, GDN_V_HEADS)
        a_raw = u[..., b_end:].astype(f32).reshape(B, T, 2, GDN_V_HEADS)
        g = -jnp.exp(a_log.astype(f32)) * jax.nn.softplus(a_raw + dt_bias.astype(f32))
        return q, k, v, z, beta, g

    def fl(t):
        return jnp.flip(t, 1)

    def scan_both(q, k, v, beta, g, S_f, S_b):
        Sf, o_f = gated_delta_chunked(q, k, v, g[:, :, 0], beta[:, :, 0], S_f)
        Sb, o_b = gated_delta_chunked(fl(q), fl(k), fl(v), fl(g[:, :, 1]), fl(beta[:, :, 1]), S_b)
        return Sf, Sb, o_f + fl(o_b)

    def finish(o, z):
        y = rms_norm(o, norm_g).astype(z.dtype) * jax.nn.silu(z)
        return y.reshape(y.shape[:2] + (GDN_V_DIM,)) @ w_out

    qc, kc, vc, zc, bc, gc = prep(h_ctx)
    S0 = jnp.zeros((h_ctx.shape[0], GDN_V_HEADS, GDN_HEAD_K, GDN_HEAD_V), f32)
    Sf, Sb, oc = scan_both(qc, kc, vc, bc, gc, S0, S0)
    ql, kl, vl, zl, bl, gl = prep(h_lat)
    _, _, ol = scan_both(ql, kl, vl, bl, gl, Sf, Sb)
    y_ctx = finish(oc, zc) if need_ctx else None
    return y_ctx, finish(ol, zl)


def ec_moe(h, router, w13, w2):
    B, N, D = h.shape
    cap = EC_FACTOR * N // N_EXPERTS
    aff = jax.nn.softmax((h @ router).astype(jnp.float32), axis=-1)
    gate, idx = lax.top_k(jnp.swapaxes(aff, 1, 2), cap)
    flat_idx = idx + (jnp.arange(B) * N)[:, None, None]
    hf = h.reshape(B * N, D)
    xin = hf[flat_idx]
    gu = jnp.einsum('becd,edf->becf', xin, w13)
    act = jax.nn.silu(gu[..., :D_EXPERT]) * gu[..., D_EXPERT:]
    y = jnp.einsum('becf,efd->becd', act, w2) * gate[..., None].astype(h.dtype)
    out = jnp.zeros_like(hf).at[flat_idx.reshape(-1)].add(y.reshape(-1, D))
    return out.reshape(B, N, D)


def setup_inputs(seed: int = 0) -> dict:
    key = jax.random.key(seed)
    keys = list(jax.random.split(key, 40))
    f32 = jnp.float32
    D = D_MODEL
    n_rw = (DEPTH + 2) // N_MIXERS
    n_mla = (DEPTH + 1) // N_MIXERS
    n_gdn = DEPTH // N_MIXERS

    def nrm(shape, scale):
        return jax.random.normal(keys.pop(), shape, f32) * scale

    def unif(shape, lo, hi):
        return jax.random.uniform(keys.pop(), shape, f32, lo, hi)

    vec_base = jnp.array([0.85, 1.0, 0.0, 1.0, 0.0], f32)[None, :, None]
    vec_noise = jnp.array([0.05, 0.05, 0.1, 0.05, 0.05], f32)[None, :, None]
    return {
        'x': nrm((BATCH, SEQ, D), 1.0),
        'c': nrm((BATCH, D), 1.0),
        'ctx': nrm((BATCH, CTX_LEN, D), 1.0),
        'c_ctx': nrm((D,), 1.0),
        'mod_w': nrm((DEPTH, D, 6 * D), 0.5 * D ** -0.5),
        'mod_b': nrm((DEPTH, 6 * D), 0.02),
        'norm_g': 1.0 + nrm((DEPTH, 2, D), 0.05),
        'rw_mu': unif((n_rw, 6, D), 0.0, 1.0),
        'rw_w_rkv': nrm((n_rw, 3, D, D), D ** -0.5),
        'rw_w_o': nrm((n_rw, D, D), D ** -0.5),
        'rw_w0': unif((n_rw, 2, D), -3.0, 1.0),
        'rw_w1': nrm((n_rw, 2, D, RW_DECAY_LORA), D ** -0.5),
        'rw_w2': nrm((n_rw, 2, RW_DECAY_LORA, D), 0.5 * RW_DECAY_LORA ** -0.5),
        'rw_a0': nrm((n_rw, 2, D), 0.1),
        'rw_a1': nrm((n_rw, 2, D, RW_AAA_LORA), D ** -0.5),
        'rw_a2': nrm((n_rw, 2, RW_AAA_LORA, D), 0.5 * RW_AAA_LORA ** -0.5),
        'rw_g1': nrm((n_rw, D, RW_GATE_LORA), D ** -0.5),
        'rw_g2': nrm((n_rw, RW_GATE_LORA, D), RW_GATE_LORA ** -0.5),
        'rw_vecs': vec_base + vec_noise * nrm((n_rw, 5, D), 1.0),
        'mla_w_in': nrm((n_mla, D, MLA_Q_LORA + MLA_KV_LORA + MLA_ROPE), D ** -0.5),
        'mla_qa_g': 1.0 + nrm((n_mla, MLA_Q_LORA), 0.05),
        'mla_kva_g': 1.0 + nrm((n_mla, MLA_KV_LORA), 0.05),
        'mla_w_qb': nrm((n_mla, MLA_Q_LORA, MLA_HEADS * MLA_QK), MLA_Q_LORA ** -0.5),
        'mla_w_kvb': nrm((n_mla, MLA_KV_LORA, MLA_HEADS * (MLA_NOPE + MLA_V)), MLA_KV_LORA ** -0.5),
        'mla_qk_g': 1.0 + nrm((n_mla, 2, MLA_QK), 0.05),
        'mla_w_o': nrm((n_mla, MLA_HEADS * MLA_V, D), (MLA_HEADS * MLA_V) ** -0.5),
        'gdn_w_in': nrm((n_gdn, D, GDN_IN), D ** -0.5),
        'gdn_conv': nrm((n_gdn, GDN_CONV, 2 * GDN_QK_DIM + GDN_V_DIM), GDN_CONV ** -0.5),
        'gdn_a_log': jnp.log(unif((n_gdn, 2, GDN_V_HEADS), 1.0, 16.0)),
        'gdn_dt_bias': nrm((n_gdn, 2, GDN_V_HEADS), 0.1),
        'gdn_norm_g': 1.0 + nrm((n_gdn, GDN_HEAD_V), 0.05),
        'gdn_w_out': nrm((n_gdn, GDN_V_DIM, D), GDN_V_DIM ** -0.5),
        'moe_router': nrm((DEPTH, D, N_EXPERTS), D ** -0.5),
        'moe_w13': nrm((DEPTH, N_EXPERTS, D, 2 * D_EXPERT), D ** -0.5),
        'moe_w2': nrm((DEPTH, N_EXPERTS, D_EXPERT, D), D_EXPERT ** -0.5),
    }


def reference(x, c, ctx, c_ctx, mod_w, mod_b, norm_g,
              rw_mu, rw_w_rkv, rw_w_o, rw_w0, rw_w1, rw_w2, rw_a0, rw_a1, rw_a2, rw_g1, rw_g2, rw_vecs,
              mla_w_in, mla_qa_g, mla_kva_g, mla_w_qb, mla_w_kvb, mla_qk_g, mla_w_o,
              gdn_w_in, gdn_conv, gdn_a_log, gdn_dt_bias, gdn_norm_g, gdn_w_out,
              moe_router, moe_w13, moe_w2):
    for i in range(DEPTH):
        last = i == DEPTH - 1
        need_ctx = not last
        m_lat = jax.nn.silu(c) @ mod_w[i] + mod_b[i]
        m_ctx = jax.nn.silu(c_ctx)[None] @ mod_w[i] + mod_b[i]
        sl = jnp.split(m_lat[:, None, :], 6, axis=-1)
        sc = jnp.split(m_ctx[:, None, :], 6, axis=-1)
        h_lat = rms_norm(x, norm_g[i, 0]) * (1.0 + sl[1]) + sl[0]
        h_ctx = rms_norm(ctx, norm_g[i, 0]) * (1.0 + sc[1]) + sc[0]
        kind = i % N_MIXERS
        j = i // N_MIXERS
        if kind == 0:
            y_ctx, y_lat = rwkv_mixer(h_ctx, h_lat, rw_mu[j], rw_w_rkv[j], rw_w_o[j], rw_w0[j], rw_w1[j], rw_w2[j],
                                      rw_a0[j], rw_a1[j], rw_a2[j], rw_g1[j], rw_g2[j], rw_vecs[j], need_ctx)
        elif kind == 1:
            y_ctx, y_lat = mla_mixer(h_ctx, h_lat, mla_w_in[j], mla_qa_g[j], mla_kva_g[j], mla_w_qb[j],
                                     mla_w_kvb[j], mla_qk_g[j], mla_w_o[j], need_ctx)
        else:
            y_ctx, y_lat = gdn_mixer(h_ctx, h_lat, gdn_w_in[j], gdn_conv[j], gdn_a_log[j], gdn_dt_bias[j],
                                     gdn_norm_g[j], gdn_w_out[j], need_ctx)
        x = x + sl[2] * y_lat
        h = rms_norm(x, norm_g[i, 1]) * (1.0 + sl[4]) + sl[3]
        x = x + sl[5] * ec_moe(h, moe_router[i], moe_w13[i], moe_w2[i])
        if need_ctx:
            ctx = ctx + sc[2] * y_ctx
            hc = rms_norm(ctx, norm_g[i, 1]) * (1.0 + sc[4]) + sc[3]
            ctx = ctx + sc[5] * ec_moe(hc, moe_router[i], moe_w13[i], moe_w2[i])
    return x
```

```python
import functools

import jax
import jax.numpy as jnp
from jax import lax
from jax.experimental import pallas as pl
from jax.experimental.pallas import tpu as pltpu

F32 = jnp.float32
BF16 = jnp.bfloat16

GRID_W = 64
N_MIXERS = 3
NORM_EPS = 1e-6
RW_HEAD = 64
RW_GN_EPS = 64e-5
MLA_HEADS = 8
MLA_Q_LORA = 384
MLA_KV_LORA = 256
MLA_NOPE = 128
MLA_ROPE = 64
MLA_V = 128
MLA_QK = MLA_NOPE + MLA_ROPE
ROPE_THETA = 10000.0
GDN_K_HEADS = 8
GDN_V_HEADS = 16
GDN_HEAD_K = 128
GDN_HEAD_V = 128
GDN_CONV = 5
GDN_QK_DIM = GDN_K_HEADS * GDN_HEAD_K
GDN_V_DIM = GDN_V_HEADS * GDN_HEAD_V
N_EXPERTS = 16
EC_FACTOR = 2

SCAN_CHUNK = 64
VMEM_LIMIT_BYTES = 48 * 1024 * 1024


def _bdot(a, b, dims):
    return lax.dot_general(a.astype(BF16), b.astype(BF16), (dims, ((), ())),
                           preferred_element_type=F32)


def _dot_nn(a, b):
    return _bdot(a, b, ((1,), (0,)))


def _dot_nt(a, b):
    return _bdot(a, b, ((1,), (1,)))


def _dot_tn(a, b):
    return _bdot(a, b, ((0,), (0,)))


def _split3(x):
    hi = x.astype(BF16)
    r1 = x - hi.astype(F32)
    mid = r1.astype(BF16)
    lo = (r1 - mid.astype(F32)).astype(BF16)
    return hi, mid, lo


def _dplr_kernel(q_ref, lw_ref, k_ref, v_ref, a_ref, b_ref, o_ref, state_ref, *, hb, reps,
                 scalar_decay):
    d = pl.program_id(2)
    c = pl.program_id(3)
    C = o_ref.shape[-2]

    @pl.when(c == 0)
    def _():
        state_ref[...] = jnp.zeros_like(state_ref)

    row = lax.broadcasted_iota(jnp.int32, (C, C), 0)
    col = lax.broadcasted_iota(jnp.int32, (C, C), 1)
    lag = (row - col) * (1 - 2 * d)
    incl = lag >= 0
    strict = lag > 0
    eye = jnp.where(row == col, 1.0, 0.0).astype(F32)
    tri = jnp.where(incl, 1.0, 0.0).astype(F32)
    K = k_ref.shape[-1]
    lane0 = jnp.where(lax.broadcasted_iota(jnp.int32, (C, K), 1) == 0, 1.0, 0.0).astype(F32)

    rq, rlw, rk, rv, ra, rb = reps
    for i in range(hb):
        q = q_ref[i // rq]
        lw = lw_ref[i // rlw]
        k = k_ref[i // rk]
        v = v_ref[i // rv]
        a = a_ref[i // ra]
        b = b_ref[i // rb]

        hi, mid, lo = _split3(lw)
        cs = (_dot_nn(tri, hi) + _dot_nn(tri, mid)) + _dot_nn(tri, lo)
        total = jnp.sum(lw, axis=0, keepdims=True)
        ex = cs - lw
        w_end = jnp.exp(total - cs)
        at = a * jnp.exp(ex)
        qt = q * jnp.exp(cs)

        if scalar_decay:
            c_hi, c_mid, c_lo = _split3(cs)
            cs_src = (_dot_nt(lane0, c_hi) + _dot_nt(lane0, c_mid)) + _dot_nt(lane0, c_lo)
            d_strict = jnp.exp(jnp.where(strict, ex[:, :C] - cs_src, -1e30))
            d_incl = jnp.exp(jnp.where(incl, cs[:, :C] - cs_src, -1e30))
            a_ab = _dot_nt(a, b) * d_strict
            a_ak = _dot_nt(a, k) * d_strict
            a_qb = _dot_nt(q, b) * d_incl
            a_qk = _dot_nt(q, k) * d_incl
        else:
            w_inv = jnp.exp(-cs)
            bt = b * w_inv
            kt = k * w_inv
            a_ab = jnp.where(strict, _dot_nt(at, bt), 0.0)
            a_ak = jnp.where(strict, _dot_nt(at, kt), 0.0)
            a_qb = jnp.where(incl, _dot_nt(qt, bt), 0.0)
            a_qk = jnp.where(incl, _dot_nt(qt, kt), 0.0)

        inv = eye + a_ab
        apow = a_ab
        steps = 1
        while steps * 2 < C:
            apow = _dot_nn(apow, apow)
            inv = inv + _dot_nn(inv, apow)
            steps *= 2

        s0 = state_ref[i]
        rhs = _dot_nt(at, s0) + _dot_nn(a_ak, v)
        u = _dot_nn(inv, rhs)
        o = _dot_nt(qt, s0) + _dot_nn(a_qb, u) + _dot_nn(a_qk, v)
        o_ref[i] = o.astype(o_ref.dtype)
        state_ref[i] = (s0 * jnp.exp(total)
                        + _dot_tn(u, b * w_end) + _dot_tn(v, k * w_end))


def _dplr_scan(q, lw, k, v, a, b, *, n_ctx, heads_per_step, scalar_decay):
    ops = (q, lw, k, v, a, b)
    B, H, T, V = v.shape[1], v.shape[2], v.shape[3], v.shape[4]
    C = SCAN_CHUNK
    hb = heads_per_step
    assert T % C == 0 and n_ctx % C == 0 and H % hb == 0
    nc = T // C
    nc_ctx = n_ctx // C
    K = k.shape[-1]

    def chunk_index(d, c):
        rev = jnp.where(c < nc_ctx, nc_ctx - 1 - c, nc + nc_ctx - 1 - c)
        return jnp.where(d == 0, c, rev)

    reps, specs = [], []
    for x in ops:
        assert H % x.shape[2] == 0
        rep = H // x.shape[2]
        assert hb % rep == 0
        reps.append(rep)
        per_dir = x.shape[0] == 2
        specs.append(pl.BlockSpec(
            (None, None, hb // rep, C, x.shape[-1]),
            functools.partial(
                lambda bi, hg, d, c, per_dir: (d if per_dir else 0, bi, hg, chunk_index(d, c), 0),
                per_dir=per_dir)))
    out_spec = pl.BlockSpec((None, None, hb, C, V),
                            lambda bi, hg, d, c: (d, bi, hg, chunk_index(d, c), 0))
    return pl.pallas_call(
        functools.partial(_dplr_kernel, hb=hb, reps=tuple(reps), scalar_decay=scalar_decay),
        grid=(B, H // hb, 2, nc),
        in_specs=specs,
        out_specs=out_spec,
        out_shape=jax.ShapeDtypeStruct((2, B, H, T, V), F32),
        scratch_shapes=[pltpu.VMEM((hb, V, K), F32)],
        compiler_params=pltpu.CompilerParams(
            dimension_semantics=("parallel", "parallel", "arbitrary", "arbitrary"),
            vmem_limit_bytes=VMEM_LIMIT_BYTES),
        name="dplr_scan",
    )(*ops)


def _attention_kernel(q_ref, k_ref, v_ref, o_ref, *, scale):
    s = _dot_nt(q_ref[...], k_ref[...]) * scale
    m = jnp.max(s, axis=-1, keepdims=True)
    p = jnp.exp(s - m)
    l = jnp.sum(p, axis=-1, keepdims=True)
    o = _dot_nn(p, v_ref[...]) / l
    o_ref[...] = o.astype(o_ref.dtype)


def _attention(q, k, v, *, scale, block_q):
    B, H, Tq, dk = q.shape
    Tk, dv = k.shape[2], v.shape[3]
    tq = min(block_q, Tq)
    assert Tq % tq == 0
    return pl.pallas_call(
        functools.partial(_attention_kernel, scale=scale),
        grid=(B, H, Tq // tq),
        in_specs=[
            pl.BlockSpec((None, None, tq, dk), lambda b, h, i: (b, h, i, 0)),
            pl.BlockSpec((None, None, Tk, dk), lambda b, h, i: (b, h, 0, 0)),
            pl.BlockSpec((None, None, Tk, dv), lambda b, h, i: (b, h, 0, 0)),
        ],
        out_specs=pl.BlockSpec((None, None, tq, dv), lambda b, h, i: (b, h, i, 0)),
        out_shape=jax.ShapeDtypeStruct((B, H, Tq, dv), F32),
        compiler_params=pltpu.CompilerParams(
            dimension_semantics=("parallel", "parallel", "arbitrary"),
            vmem_limit_bytes=VMEM_LIMIT_BYTES),
        name="mla_attention",
    )(q, k, v)


def _moe_ffn_kernel(x_ref, wg_ref, wu_ref, wd_ref, o_ref):
    f = pl.program_id(2)
    x = x_ref[...]
    g = _dot_nn(x, wg_ref[...])
    u = _dot_nn(x, wu_ref[...])
    act = (g * jax.nn.sigmoid(g)) * u
    y = _dot_nn(act, wd_ref[...])

    @pl.when(f == 0)
    def _():
        o_ref[...] = y

    @pl.when(f != 0)
    def _():
        o_ref[...] += y


def _moe_ffn(x, w13, w2, *, block_m, block_f):
    E, M, D = x.shape
    Fh = w2.shape[1]
    tm = min(block_m, M)
    tf = min(block_f, Fh)
    assert M % tm == 0 and Fh % tf == 0
    nf = Fh // tf
    return pl.pallas_call(
        _moe_ffn_kernel,
        grid=(E, M // tm, nf),
        in_specs=[
            pl.BlockSpec((None, tm, D), lambda e, m, f: (e, m, 0)),
            pl.BlockSpec((None, D, tf), lambda e, m, f: (e, 0, f)),
            pl.BlockSpec((None, D, tf), lambda e, m, f: (e, 0, nf + f)),
            pl.BlockSpec((None, tf, D), lambda e, m, f: (e, f, 0)),
        ],
        out_specs=pl.BlockSpec((None, tm, D), lambda e, m, f: (e, m, 0)),
        out_shape=jax.ShapeDtypeStruct((E, M, D), F32),
        compiler_params=pltpu.CompilerParams(
            dimension_semantics=("parallel", "parallel", "arbitrary"),
            vmem_limit_bytes=VMEM_LIMIT_BYTES),
        name="moe_ffn",
    )(x, w13, w13, w2)


def _rms_norm(x, g, eps=NORM_EPS):
    return x * lax.rsqrt(jnp.mean(x * x, -1, keepdims=True) + eps) * g


def _l2_normalize(x, eps=1e-6):
    return x * lax.rsqrt(jnp.sum(x * x, -1, keepdims=True) + eps)


def _centred_shift(h):
    prev = jnp.pad(h[:, :-1], ((0, 0), (1, 0), (0, 0)))
    nxt = jnp.pad(h[:, 1:], ((0, 0), (0, 1), (0, 0)))
    return 0.5 * (prev + nxt) - h


def _to_heads(t, heads):
    B, T, _ = t.shape
    return jnp.swapaxes(t.reshape(B, T, heads, -1), 1, 2)


def _from_heads(t):
    B, H, T, dim = t.shape
    return jnp.swapaxes(t, 1, 2).reshape(B, T, H * dim)


def _rwkv_mixer(h_ctx, h_lat, mu, w_rkv, w_o, w0, w1, w2, a0, a1, a2, g1, g2, vecs, need_ctx):
    k_k, k_a, r_k, ln_w, ln_b = (vecs[j] for j in range(5))
    D = h_lat.shape[-1]
    H = D // RW_HEAD
    n_ctx = h_ctx.shape[1]

    def prep(h):
        xx = _centred_shift(h)
        xr, xw, xk, xv, xa, xg = (h + xx * mu[j] for j in range(6))
        r = xr @ w_rkv[0]
        k = xk @ w_rkv[1]
        v = xv @ w_rkv[2]
        gate = jax.nn.sigmoid(xg @ g1) @ g2
        kk = _from_heads(_l2_normalize(_to_heads(k * k_k, H)))
        per_dir = []
        for d in range(2):
            w_log = -jax.nn.softplus(-(w0[d] + jnp.tanh(xw @ w1[d]) @ w2[d])) - 0.5
            lw = -jnp.exp(w_log)
            a = jax.nn.sigmoid(a0[d] + (xa @ a1[d]) @ a2[d])
            k_d = k * (1.0 + (a - 1.0) * k_a)
            per_dir.append((lw, k_d, kk * a))
        return r, v, kk, gate, per_dir

    pc = prep(h_ctx)
    pL = prep(h_lat)

    def cat(xc, xl):
        return _to_heads(jnp.concatenate([xc, xl], axis=1), H)

    r = cat(pc[0], pL[0])
    v = cat(pc[1], pL[1])
    kk = cat(pc[2], pL[2])
    lw = jnp.stack([cat(pc[4][d][0], pL[4][d][0]) for d in range(2)])
    kd = jnp.stack([cat(pc[4][d][1], pL[4][d][1]) for d in range(2)])
    bb = jnp.stack([cat(pc[4][d][2], pL[4][d][2]) for d in range(2)])
    o = _dplr_scan(r[None], lw, kd, v[None], (-kk)[None], bb, n_ctx=n_ctx, heads_per_step=8,
                   scalar_decay=False)
    o_sum = o[0] + o[1]
    bonus = jnp.sum(r * (kd[0] + kd[1]) * _to_heads(r_k[None, None, :], H), -1, keepdims=True) * v

    mean = jnp.mean(o_sum, -1, keepdims=True)
    var = jnp.mean(jnp.square(o_sum - mean), -1, keepdims=True)
    y = (o_sum - mean) * lax.rsqrt(var + RW_GN_EPS)
    y = y * _to_heads(ln_w[None, None, :], H) + _to_heads(ln_b[None, None, :], H) + bonus
    y = _from_heads(y)
    gate = jnp.concatenate([pc[3], pL[3]], axis=1)
    if need_ctx:
        out = (y * gate) @ w_o
        return out[:, :n_ctx], out[:, n_ctx:]
    return None, (y[:, n_ctx:] * gate[:, n_ctx:]) @ w_o


def _axial_rope(t, n):
    rows = n // GRID_W
    row_id = jnp.repeat(jnp.arange(rows), GRID_W)
    col_id = jnp.tile(jnp.arange(GRID_W), rows)
    nf = MLA_ROPE // 4
    inv = ROPE_THETA ** (-jnp.arange(nf, dtype=F32) / nf)

    def rot(seg, pos):
        ang = pos.astype(F32)[:, None] * inv
        cos = jnp.cos(ang)[:, None, :]
        sin = jnp.sin(ang)[:, None, :]
        s1, s2 = seg[..., :nf], seg[..., nf:]
        return jnp.concatenate([s1 * cos - s2 * sin, s2 * cos + s1 * sin], -1)

    half = MLA_ROPE // 2
    return jnp.concatenate([rot(t[..., :half], row_id), rot(t[..., half:], col_id)], -1)


def _mla_project(h, w_in, qa_g, kva_g, w_qb, w_kvb, qk_g):
    B, T, _ = h.shape
    u = h @ w_in
    cq = _rms_norm(u[..., :MLA_Q_LORA], qa_g)
    ckv = _rms_norm(u[..., MLA_Q_LORA:MLA_Q_LORA + MLA_KV_LORA], kva_g)
    k_pe = u[..., MLA_Q_LORA + MLA_KV_LORA:]
    q = (cq @ w_qb).reshape(B, T, MLA_HEADS, MLA_QK)
    kv = (ckv @ w_kvb).reshape(B, T, MLA_HEADS, MLA_NOPE + MLA_V)
    k_nope, v = kv[..., :MLA_NOPE], kv[..., MLA_NOPE:]
    k = jnp.concatenate([k_nope, jnp.broadcast_to(k_pe[:, :, None, :], (B, T, MLA_HEADS, MLA_ROPE))], -1)
    return _rms_norm(q, qk_g[0]), _rms_norm(k, qk_g[1]), v


def _mla_mixer(h_ctx, h_lat, w_in, qa_g, kva_g, w_qb, w_kvb, qk_g, w_o, need_ctx):
    B, n, _ = h_lat.shape
    qc, kc, vc = _mla_project(h_ctx, w_in, qa_g, kva_g, w_qb, w_kvb, qk_g)
    ql, kl, vl = _mla_project(h_lat, w_in, qa_g, kva_g, w_qb, w_kvb, qk_g)
    ql = jnp.concatenate([ql[..., :MLA_NOPE], _axial_rope(ql[..., MLA_NOPE:], n)], -1)
    kl = jnp.concatenate([kl[..., :MLA_NOPE], _axial_rope(kl[..., MLA_NOPE:], n)], -1)
    k_all = jnp.concatenate([kl, kc], 1)
    v_all = jnp.concatenate([vl, vc], 1)

    def hm(t):
        return jnp.swapaxes(t, 1, 2).astype(BF16)

    scale = MLA_QK ** -0.5
    o_lat = _attention(hm(ql), hm(k_all), hm(v_all), scale=scale, block_q=128)
    o_lat = jnp.swapaxes(o_lat, 1, 2).reshape(B, n, MLA_HEADS * MLA_V) @ w_o
    y_ctx = None
    if need_ctx:
        o_ctx = _attention(hm(qc), hm(kc), hm(vc), scale=scale, block_q=128)
        y_ctx = jnp.swapaxes(o_ctx, 1, 2).reshape(B, h_ctx.shape[1], MLA_HEADS * MLA_V) @ w_o
    return y_ctx, o_lat


def _centred_depthwise_conv(u, w):
    ch = u.shape[-1]
    return lax.conv_general_dilated(u, w[:, None, :].astype(u.dtype), window_strides=(1,),
                                    padding=[(GDN_CONV // 2, GDN_CONV // 2)],
                                    dimension_numbers=('NWC', 'WIO', 'NWC'), feature_group_count=ch)


def _gdn_mixer(h_ctx, h_lat, w_in, conv_w, a_log, dt_bias, norm_g, w_out, need_ctx):
    qk_end = 2 * GDN_QK_DIM
    v_end = qk_end + GDN_V_DIM
    z_end = v_end + GDN_V_DIM
    b_end = z_end + 2 * GDN_V_HEADS
    n_ctx = h_ctx.shape[1]

    def prep(h):
        B, T, _ = h.shape
        u = h @ w_in
        qkv = jax.nn.silu(_centred_depthwise_conv(u[..., :v_end], conv_w))
        q = _l2_normalize(qkv[..., :GDN_QK_DIM].reshape(B, T, GDN_K_HEADS, GDN_HEAD_K))
        k = _l2_normalize(qkv[..., GDN_QK_DIM:qk_end].reshape(B, T, GDN_K_HEADS, GDN_HEAD_K))
        v = qkv[..., qk_end:].reshape(B, T, GDN_V_HEADS, GDN_HEAD_V)
        z = u[..., v_end:z_end]
        beta = jax.nn.sigmoid(u[..., z_end:b_end]).reshape(B, T, 2, GDN_V_HEADS)
        a_raw = u[..., b_end:].reshape(B, T, 2, GDN_V_HEADS)
        g = -jnp.exp(a_log) * jax.nn.softplus(a_raw + dt_bias)
        return q, k, v, z, beta, g

    pc = prep(h_ctx)
    pL = prep(h_lat)
    q, k, v, z, beta, g = (jnp.concatenate([xc, xl], axis=1) for xc, xl in zip(pc, pL))
    B, T = q.shape[0], q.shape[1]
    rep = GDN_V_HEADS // GDN_K_HEADS
    qh = jnp.swapaxes(q, 1, 2) * (GDN_HEAD_K ** -0.5)
    kh = jnp.swapaxes(k, 1, 2)
    vh = jnp.swapaxes(v, 1, 2)
    beta_h = jnp.moveaxis(beta, (2, 3), (0, 2))[..., None]
    g_h = jnp.moveaxis(g, (2, 3), (0, 2))[..., None]
    k_rep = jnp.repeat(kh, rep, axis=1)[None]
    lw = jnp.broadcast_to(g_h, (2, B, GDN_V_HEADS, T, GDN_HEAD_K))
    b_op = -(beta_h * jnp.exp(g_h)) * k_rep
    v_op = beta_h * vh[None]
    o = _dplr_scan(qh[None], lw, kh[None], v_op, kh[None], b_op, n_ctx=n_ctx, heads_per_step=4,
                   scalar_decay=True)
    o = jnp.swapaxes(o[0] + o[1], 1, 2)

    y = _rms_norm(o, norm_g) * jax.nn.silu(z.reshape(B, T, GDN_V_HEADS, GDN_HEAD_V))
    y = y.reshape(B, T, GDN_V_DIM)
    if need_ctx:
        out = y @ w_out
        return out[:, :n_ctx], out[:, n_ctx:]
    return None, y[:, n_ctx:] @ w_out


def _ec_moe(h, router, w13, w2):
    B, N, D = h.shape
    E = router.shape[-1]
    cap = EC_FACTOR * N // E
    aff = jax.nn.softmax(h @ router, axis=-1)
    gate, idx = lax.top_k(jnp.swapaxes(aff, 1, 2), cap)
    flat_idx = idx + (jnp.arange(B) * N)[:, None, None]
    hf = h.reshape(B * N, D)
    xin = hf.astype(BF16)[jnp.swapaxes(flat_idx, 0, 1).reshape(E, B * cap)]
    y = _moe_ffn(xin, w13, w2, block_m=1024, block_f=256)
    y = y.reshape(E, B, cap, D) * jnp.swapaxes(gate, 0, 1)[..., None]
    out = jnp.zeros_like(hf).at[jnp.swapaxes(flat_idx, 0, 1).reshape(-1)].add(y.reshape(-1, D))
    return out.reshape(B, N, D)


def kernel(x, c, ctx, c_ctx, mod_w, mod_b, norm_g, rw_mu, rw_w_rkv, rw_w_o, rw_w0, rw_w1, rw_w2, rw_a0, rw_a1, rw_a2, rw_g1, rw_g2, rw_vecs, mla_w_in, mla_qa_g, mla_kva_g, mla_w_qb, mla_w_kvb, mla_qk_g, mla_w_o, gdn_w_in, gdn_conv, gdn_a_log, gdn_dt_bias, gdn_norm_g, gdn_w_out, moe_router, moe_w13, moe_w2):
    depth = mod_w.shape[0]
    for i in range(depth):
        need_ctx = i != depth - 1
        m_lat = jax.nn.silu(c) @ mod_w[i] + mod_b[i]
        m_ctx = jax.nn.silu(c_ctx)[None] @ mod_w[i] + mod_b[i]
        sl = jnp.split(m_lat[:, None, :], 6, axis=-1)
        sc = jnp.split(m_ctx[:, None, :], 6, axis=-1)
        h_lat = _rms_norm(x, norm_g[i, 0]) * (1.0 + sl[1]) + sl[0]
        h_ctx = _rms_norm(ctx, norm_g[i, 0]) * (1.0 + sc[1]) + sc[0]
        kind = i % N_MIXERS
        j = i // N_MIXERS
        if kind == 0:
            y_ctx, y_lat = _rwkv_mixer(h_ctx, h_lat, rw_mu[j], rw_w_rkv[j], rw_w_o[j], rw_w0[j], rw_w1[j],
                                       rw_w2[j], rw_a0[j], rw_a1[j], rw_a2[j], rw_g1[j], rw_g2[j],
                                       rw_vecs[j], need_ctx)
        elif kind == 1:
            y_ctx, y_lat = _mla_mixer(h_ctx, h_lat, mla_w_in[j], mla_qa_g[j], mla_kva_g[j], mla_w_qb[j],
                                      mla_w_kvb[j], mla_qk_g[j], mla_w_o[j], need_ctx)
        else:
            y_ctx, y_lat = _gdn_mixer(h_ctx, h_lat, gdn_w_in[j], gdn_conv[j], gdn_a_log[j], gdn_dt_bias[j],
                                      gdn_norm_g[j], gdn_w_out[j], need_ctx)
        x = x + sl[2] * y_lat
        h = _rms_norm(x, norm_g[i, 1]) * (1.0 + sl[4]) + sl[3]
        x = x + sl[5] * _ec_moe(h, moe_router[i], moe_w13[i], moe_w2[i])
        if need_ctx:
            ctx = ctx + sc[2] * y_ctx
            hc = _rms_norm(ctx, norm_g[i, 1]) * (1.0 + sc[4]) + sc[3]
            ctx = ctx + sc[5] * _ec_moe(hc, moe_router[i], moe_w13[i], moe_w2[i])
    return x
```

```python
import functools

import jax
import jax.numpy as jnp
from jax import lax
from jax.experimental import pallas as pl
from jax.experimental.pallas import tpu as pltpu

F32 = jnp.float32
BF16 = jnp.bfloat16

GRID_W = 64
N_MIXERS = 3
NORM_EPS = 1e-6
RW_HEAD = 64
RW_GN_EPS = 64e-5
MLA_HEADS = 8
MLA_Q_LORA = 384
MLA_KV_LORA = 256
MLA_NOPE = 128
MLA_ROPE = 64
MLA_V = 128
MLA_QK = MLA_NOPE + MLA_ROPE
ROPE_THETA = 10000.0
GDN_K_HEADS = 8
GDN_V_HEADS = 16
GDN_HEAD_K = 128
GDN_HEAD_V = 128
GDN_CONV = 5
GDN_QK_DIM = GDN_K_HEADS * GDN_HEAD_K
GDN_V_DIM = GDN_V_HEADS * GDN_HEAD_V
N_EXPERTS = 16
EC_FACTOR = 2

LANES = 128
SCAN_CHUNK = 64
VMEM_LIMIT_BYTES = 48 * 1024 * 1024


def _bdot(a, b, dims):
    return lax.dot_general(a.astype(BF16), b.astype(BF16), (dims, ((), ())),
                           preferred_element_type=F32)


def _dot_nn(a, b):
    return _bdot(a, b, ((1,), (0,)))


def _dot_nt(a, b):
    return _bdot(a, b, ((1,), (1,)))


def _dot_tn(a, b):
    return _bdot(a, b, ((0,), (0,)))


def _split3(x):
    hi = x.astype(BF16)
    r1 = x - hi.astype(F32)
    mid = r1.astype(BF16)
    lo = (r1 - mid.astype(F32)).astype(BF16)
    return hi, mid, lo


def _dplr_kernel(q_ref, lw_ref, k_ref, v_ref, a_ref, b_ref, o_ref, state_ref, *, groups, hpl, reps,
                 scalar_decay):
    d = pl.program_id(2)
    c = pl.program_id(3)
    C = o_ref.shape[-2]
    hC = hpl * C
    G = range(groups)

    @pl.when(c == 0)
    def _():
        state_ref[...] = jnp.zeros_like(state_ref)

    assert C & (C - 1) == 0 and hpl & (hpl - 1) == 0
    log_c = C.bit_length() - 1
    head_w = LANES // hpl
    log_hw = head_w.bit_length() - 1

    def time_masks(n):
        row = lax.broadcasted_iota(jnp.int32, (n, n), 0)
        col = lax.broadcasted_iota(jnp.int32, (n, n), 1)
        lag = ((row & (C - 1)) - (col & (C - 1))) * (1 - 2 * d)
        same = (row >> log_c) == (col >> log_c)
        return same & (lag >= 0), same & (lag > 0), row == col

    incl_c, _, _ = time_masks(C)
    incl, strict, diag = time_masks(hC)
    eye = jnp.where(diag, 1.0, 0.0).astype(F32)
    tri = jnp.where(incl_c, 1.0, 0.0).astype(F32)
    lane = lax.broadcasted_iota(jnp.int32, (1, LANES), 1)
    lane_masks = [(lane >> log_hw) == j for j in range(hpl)]

    def stack(x):
        if hpl == 1:
            return x
        return jnp.concatenate([jnp.where(m, x, 0.0) for m in lane_masks], axis=0)

    def fold(x):
        out = x[0:C]
        for j in range(1, hpl):
            out = out + x[j * C:(j + 1) * C]
        return out

    def grp(ref, g, rep):
        j = g // rep
        return ref[:, j * LANES:(j + 1) * LANES]

    rq, rlw, rk, rv, ra, rb = reps
    q = [grp(q_ref, g, rq) for g in G]
    lw = [grp(lw_ref, g, rlw) for g in G]
    k = [grp(k_ref, g, rk) for g in G]
    v = [grp(v_ref, g, rv) for g in G]
    a = [grp(a_ref, g, ra) for g in G]
    b = [grp(b_ref, g, rb) for g in G]

    parts = [_split3(lw[g]) for g in G]
    cs = [(_dot_nn(tri, parts[g][0]) + _dot_nn(tri, parts[g][1])) + _dot_nn(tri, parts[g][2]) for g in G]
    total = [jnp.sum(lw[g], axis=0, keepdims=True) for g in G]
    ex = [cs[g] - lw[g] for g in G]
    w_end = [jnp.exp(total[g] - cs[g]) for g in G]
    at = [a[g] * jnp.exp(ex[g]) for g in G]
    qt = [q[g] * jnp.exp(cs[g]) for g in G]
    v_st = [stack(v[g]) for g in G]

    if scalar_decay:
        assert hpl == 1
        lane0 = jnp.where(lax.broadcasted_iota(jnp.int32, (C, LANES), 1) == 0, 1.0, 0.0).astype(F32)
        cparts = [_split3(cs[g]) for g in G]
        cs_src = [(_dot_nt(lane0, cparts[g][0]) + _dot_nt(lane0, cparts[g][1]))
                  + _dot_nt(lane0, cparts[g][2]) for g in G]
        d_strict = [jnp.exp(jnp.where(strict, ex[g][:, :C] - cs_src[g], -1e30)) for g in G]
        d_incl = [jnp.exp(jnp.where(incl, cs[g][:, :C] - cs_src[g], -1e30)) for g in G]
        lhs = [jnp.concatenate([a[g], q[g]], axis=0) for g in G]
        xb = [_dot_nt(lhs[g], b[g]) for g in G]
        xk = [_dot_nt(lhs[g], k[g]) for g in G]
        a_ab = [xb[g][:hC] * d_strict[g] for g in G]
        a_qb = [xb[g][hC:] * d_incl[g] for g in G]
        a_ak = [xk[g][:hC] * d_strict[g] for g in G]
        a_qk = [xk[g][hC:] * d_incl[g] for g in G]
    else:
        w_inv = [jnp.exp(-cs[g]) for g in G]
        lhs = [jnp.concatenate([stack(at[g]), stack(qt[g])], axis=0) for g in G]
        xb = [_dot_nt(lhs[g], stack(b[g] * w_inv[g])) for g in G]
        xk = [_dot_nt(lhs[g], stack(k[g] * w_inv[g])) for g in G]
        a_ab = [jnp.where(strict, xb[g][:hC], 0.0) for g in G]
        a_qb = [jnp.where(incl, xb[g][hC:], 0.0) for g in G]
        a_ak = [jnp.where(strict, xk[g][:hC], 0.0) for g in G]
        a_qk = [jnp.where(incl, xk[g][hC:], 0.0) for g in G]

    s0 = [state_ref[g] for g in G]
    sp = [_dot_nt(jnp.concatenate([at[g], qt[g]], axis=0), s0[g]) for g in G]

    inv = [eye + a_ab[g] for g in G]
    apow = a_ab
    steps = 1
    while steps * 2 < C:
        apow = [_dot_nn(apow[g], apow[g]) for g in G]
        inv = [inv[g] + _dot_nn(inv[g], apow[g]) for g in G]
        steps *= 2

    rhs = [stack(sp[g][:C]) + _dot_nn(a_ak[g], v_st[g]) for g in G]
    u_st = [_dot_nn(inv[g], rhs[g]) for g in G]
    o = [sp[g][C:] + fold(_dot_nn(a_qb[g], u_st[g]) + _dot_nn(a_qk[g], v_st[g])) for g in G]
    for g in G:
        o_ref[:, g * LANES:(g + 1) * LANES] = o[g].astype(o_ref.dtype)

    upd = [_dot_tn(jnp.concatenate([fold(u_st[g]), v[g]], axis=0),
                   jnp.concatenate([b[g] * w_end[g], k[g] * w_end[g]], axis=0)) for g in G]
    if hpl > 1:
        r2 = lax.broadcasted_iota(jnp.int32, (LANES, LANES), 0) >> log_hw
        c2 = lax.broadcasted_iota(jnp.int32, (LANES, LANES), 1) >> log_hw
        upd = [jnp.where(r2 == c2, upd[g], 0.0) for g in G]
    for g in G:
        state_ref[g] = s0[g] * jnp.exp(total[g]) + upd[g]


def _dplr_scan(q, lw, k, v, a, b, *, n_ctx, groups_per_step, heads_per_group, scalar_decay):
    ops = (q, lw, k, v, a, b)
    B, T, DV = v.shape[1], v.shape[2], v.shape[3]
    C = SCAN_CHUNK
    G = groups_per_step
    assert T % C == 0 and n_ctx % C == 0 and DV % (G * LANES) == 0
    nc = T // C
    nc_ctx = n_ctx // C

    def chunk_index(d, c):
        rev = jnp.where(c < nc_ctx, nc_ctx - 1 - c, nc + nc_ctx - 1 - c)
        return jnp.where(d == 0, c, rev)

    reps, specs = [], []
    for x in ops:
        assert DV % x.shape[3] == 0
        rep = DV // x.shape[3]
        assert G % rep == 0
        reps.append(rep)
        per_dir = x.shape[0] == 2
        specs.append(pl.BlockSpec(
            (None, None, C, (G // rep) * LANES),
            functools.partial(
                lambda bi, hg, d, c, per_dir: (d if per_dir else 0, bi, chunk_index(d, c), hg),
                per_dir=per_dir)))
    out_spec = pl.BlockSpec((None, None, C, G * LANES),
                            lambda bi, hg, d, c: (d, bi, chunk_index(d, c), hg))
    return pl.pallas_call(
        functools.partial(_dplr_kernel, groups=G, hpl=heads_per_group, reps=tuple(reps),
                          scalar_decay=scalar_decay),
        grid=(B, DV // (G * LANES), 2, nc),
        in_specs=specs,
        out_specs=out_spec,
        out_shape=jax.ShapeDtypeStruct((2, B, T, DV), F32),
        scratch_shapes=[pltpu.VMEM((G, LANES, LANES), F32)],
        compiler_params=pltpu.CompilerParams(
            dimension_semantics=("parallel", "parallel", "arbitrary", "arbitrary"),
            vmem_limit_bytes=VMEM_LIMIT_BYTES),
        name="dplr_scan",
    )(*ops)


def _attention_kernel(q_ref, k_ref, v_ref, o_ref, *, scale, tk):
    q = q_ref[...]
    dv = o_ref.shape[-1]
    c = scale * 1.4426950408889634
    n_chunks = k_ref.shape[0] // tk

    def scores(j):
        return _dot_nt(q, k_ref[j * tk:(j + 1) * tk, :])

    s = scores(0)
    m = acc = None
    for j in range(n_chunks):
        s_next = scores(j + 1) if j + 1 < n_chunks else None
        m_chunk = jnp.max(s, axis=-1, keepdims=True)
        m_new = m_chunk if m is None else jnp.maximum(m, m_chunk)
        p = jnp.exp2((s - m_new) * c)
        pv = _dot_nn(p, v_ref[j * tk:(j + 1) * tk, :])
        acc = pv if acc is None else jnp.exp2((m - m_new) * c) * acc + pv
        m, s = m_new, s_next
    o_ref[...] = (acc[:, :dv] / acc[:, dv:dv + 1]).astype(o_ref.dtype)


def _largest_divisor(n, candidates):
    return next(c for c in candidates if n % c == 0)


def _attention(q, k, v, *, scale, block_q):
    B, H, Tq, dk = q.shape
    Tk, dv = k.shape[2], v.shape[3]
    tq = min(block_q, Tq)
    assert Tq % tq == 0 and dv % LANES == 0
    tk = _largest_divisor(Tk, (1408, 768, 512, 256, 128, 64, 32, 16))
    v = jnp.concatenate([v, jnp.ones((B, H, Tk, LANES), v.dtype)], axis=-1)
    return pl.pallas_call(
        functools.partial(_attention_kernel, scale=scale, tk=tk),
        grid=(B, H, Tq // tq),
        in_specs=[
            pl.BlockSpec((None, None, tq, dk), lambda b, h, i: (b, h, i, 0)),
            pl.BlockSpec((None, None, Tk, dk), lambda b, h, i: (b, h, 0, 0)),
            pl.BlockSpec((None, None, Tk, dv + LANES), lambda b, h, i: (b, h, 0, 0)),
        ],
        out_specs=pl.BlockSpec((None, None, tq, dv), lambda b, h, i: (b, h, i, 0)),
        out_shape=jax.ShapeDtypeStruct((B, H, Tq, dv), F32),
        compiler_params=pltpu.CompilerParams(
            dimension_semantics=("parallel", "parallel", "arbitrary"),
            vmem_limit_bytes=VMEM_LIMIT_BYTES),
        name="mla_attention",
    )(q, k, v)


def _moe_ffn_kernel(x_ref, wg_ref, wu_ref, wd_ref, o_ref):
    f = pl.program_id(2)
    x = x_ref[...]
    g = _dot_nn(x, wg_ref[...])
    u = _dot_nn(x, wu_ref[...])
    act = (g * jax.nn.sigmoid(g)) * u
    y = _dot_nn(act, wd_ref[...])

    @pl.when(f == 0)
    def _():
        o_ref[...] = y

    @pl.when(f != 0)
    def _():
        o_ref[...] += y


def _moe_ffn(x, w13, w2, *, block_m, block_f):
    E, M, D = x.shape
    Fh = w2.shape[1]
    tm = min(block_m, M)
    tf = min(block_f, Fh)
    assert M % tm == 0 and Fh % tf == 0
    nf = Fh // tf
    return pl.pallas_call(
        _moe_ffn_kernel,
        grid=(E, M // tm, nf),
        in_specs=[
            pl.BlockSpec((None, tm, D), lambda e, m, f: (e, m, 0)),
            pl.BlockSpec((None, D, tf), lambda e, m, f: (e, 0, f)),
            pl.BlockSpec((None, D, tf), lambda e, m, f: (e, 0, nf + f)),
            pl.BlockSpec((None, tf, D), lambda e, m, f: (e, f, 0)),
        ],
        out_specs=pl.BlockSpec((None, tm, D), lambda e, m, f: (e, m, 0)),
        out_shape=jax.ShapeDtypeStruct((E, M, D), F32),
        compiler_params=pltpu.CompilerParams(
            dimension_semantics=("parallel", "parallel", "arbitrary"),
            vmem_limit_bytes=VMEM_LIMIT_BYTES),
        name="moe_ffn",
    )(x, w13, w13, w2)


def _rms_norm(x, g, eps=NORM_EPS):
    return x * lax.rsqrt(jnp.mean(x * x, -1, keepdims=True) + eps) * g


def _l2_normalize(x, eps=1e-6):
    return x * lax.rsqrt(jnp.sum(x * x, -1, keepdims=True) + eps)


def _centred_shift(h):
    prev = jnp.pad(h[:, :-1], ((0, 0), (1, 0), (0, 0)))
    nxt = jnp.pad(h[:, 1:], ((0, 0), (0, 1), (0, 0)))
    return 0.5 * (prev + nxt) - h


def _heads(t, dim):
    return t.reshape(t.shape[:-1] + (t.shape[-1] // dim, dim))


def _flat(t):
    return t.reshape(t.shape[:-2] + (t.shape[-2] * t.shape[-1],))


def _rwkv_mixer(h_ctx, h_lat, mu, w_rkv, w_o, w0, w1, w2, a0, a1, a2, g1, g2, vecs, need_ctx):
    k_k, k_a, r_k, ln_w, ln_b = (vecs[j] for j in range(5))
    n_ctx = h_ctx.shape[1]

    def prep(h):
        xx = _centred_shift(h)
        xr, xw, xk, xv, xa, xg = (h + xx * mu[j] for j in range(6))
        r = xr @ w_rkv[0]
        k = xk @ w_rkv[1]
        v = xv @ w_rkv[2]
        gate = jax.nn.sigmoid(xg @ g1) @ g2
        kk = _flat(_l2_normalize(_heads(k * k_k, RW_HEAD)))
        per_dir = []
        for d in range(2):
            w_log = -jax.nn.softplus(-(w0[d] + jnp.tanh(xw @ w1[d]) @ w2[d])) - 0.5
            lw = -jnp.exp(w_log)
            a = jax.nn.sigmoid(a0[d] + (xa @ a1[d]) @ a2[d])
            k_d = k * (1.0 + (a - 1.0) * k_a)
            per_dir.append((lw, k_d, kk * a))
        return r, v, kk, gate, per_dir

    pc = prep(h_ctx)
    pL = prep(h_lat)

    def cat(xc, xl):
        return jnp.concatenate([xc, xl], axis=1)

    r = cat(pc[0], pL[0])
    v = cat(pc[1], pL[1])
    kk = cat(pc[2], pL[2])
    gate = cat(pc[3], pL[3])
    lw = jnp.stack([cat(pc[4][d][0], pL[4][d][0]) for d in range(2)])
    kd = jnp.stack([cat(pc[4][d][1], pL[4][d][1]) for d in range(2)])
    bb = jnp.stack([cat(pc[4][d][2], pL[4][d][2]) for d in range(2)])
    o = _dplr_scan(r[None], lw, kd, v[None], (-kk)[None], bb, n_ctx=n_ctx,
                   groups_per_step=r.shape[-1] // LANES, heads_per_group=LANES // RW_HEAD,
                   scalar_decay=False)
    o_sum = _heads(o[0] + o[1], RW_HEAD)
    vh = _heads(v, RW_HEAD)
    bonus = jnp.sum(_heads(r * (kd[0] + kd[1]) * r_k, RW_HEAD), -1, keepdims=True) * vh

    mean = jnp.mean(o_sum, -1, keepdims=True)
    var = jnp.mean(jnp.square(o_sum - mean), -1, keepdims=True)
    y = (o_sum - mean) * lax.rsqrt(var + RW_GN_EPS)
    y = y * _heads(ln_w, RW_HEAD) + _heads(ln_b, RW_HEAD) + bonus
    y = _flat(y)
    if need_ctx:
        out = (y * gate) @ w_o
        return out[:, :n_ctx], out[:, n_ctx:]
    return None, (y[:, n_ctx:] * gate[:, n_ctx:]) @ w_o


def _axial_rope(t, n):
    rows = n // GRID_W
    row_id = jnp.repeat(jnp.arange(rows), GRID_W)
    col_id = jnp.tile(jnp.arange(GRID_W), rows)
    nf = MLA_ROPE // 4
    inv = ROPE_THETA ** (-jnp.arange(nf, dtype=F32) / nf)

    def rot(seg, pos):
        ang = pos.astype(F32)[:, None] * inv
        cos = jnp.cos(ang)[:, None, :]
        sin = jnp.sin(ang)[:, None, :]
        s1, s2 = seg[..., :nf], seg[..., nf:]
        return jnp.concatenate([s1 * cos - s2 * sin, s2 * cos + s1 * sin], -1)

    half = MLA_ROPE // 2
    return jnp.concatenate([rot(t[..., :half], row_id), rot(t[..., half:], col_id)], -1)


def _mla_project(h, w_in, qa_g, kva_g, w_qb, w_kvb, qk_g):
    B, T, _ = h.shape
    u = h @ w_in
    cq = _rms_norm(u[..., :MLA_Q_LORA], qa_g)
    ckv = _rms_norm(u[..., MLA_Q_LORA:MLA_Q_LORA + MLA_KV_LORA], kva_g)
    k_pe = u[..., MLA_Q_LORA + MLA_KV_LORA:]
    q = (cq @ w_qb).reshape(B, T, MLA_HEADS, MLA_QK)
    kv = (ckv @ w_kvb).reshape(B, T, MLA_HEADS, MLA_NOPE + MLA_V)
    k_nope, v = kv[..., :MLA_NOPE], kv[..., MLA_NOPE:]
    k = jnp.concatenate([k_nope, jnp.broadcast_to(k_pe[:, :, None, :], (B, T, MLA_HEADS, MLA_ROPE))], -1)
    return _rms_norm(q, qk_g[0]), _rms_norm(k, qk_g[1]), v


def _mla_mixer(h_ctx, h_lat, w_in, qa_g, kva_g, w_qb, w_kvb, qk_g, w_o, need_ctx):
    B, n, _ = h_lat.shape
    qc, kc, vc = _mla_project(h_ctx, w_in, qa_g, kva_g, w_qb, w_kvb, qk_g)
    ql, kl, vl = _mla_project(h_lat, w_in, qa_g, kva_g, w_qb, w_kvb, qk_g)
    ql = jnp.concatenate([ql[..., :MLA_NOPE], _axial_rope(ql[..., MLA_NOPE:], n)], -1)
    kl = jnp.concatenate([kl[..., :MLA_NOPE], _axial_rope(kl[..., MLA_NOPE:], n)], -1)
    k_all = jnp.concatenate([kl, kc], 1)
    v_all = jnp.concatenate([vl, vc], 1)

    def hm(t):
        return jnp.swapaxes(t, 1, 2).astype(BF16)

    scale = MLA_QK ** -0.5
    o_lat = _attention(hm(ql), hm(k_all), hm(v_all), scale=scale, block_q=512)
    o_lat = jnp.swapaxes(o_lat, 1, 2).reshape(B, n, MLA_HEADS * MLA_V) @ w_o
    y_ctx = None
    if need_ctx:
        o_ctx = _attention(hm(qc), hm(kc), hm(vc), scale=scale, block_q=512)
        y_ctx = jnp.swapaxes(o_ctx, 1, 2).reshape(B, h_ctx.shape[1], MLA_HEADS * MLA_V) @ w_o
    return y_ctx, o_lat


def _centred_depthwise_conv(u, w):
    ch = u.shape[-1]
    return lax.conv_general_dilated(u, w[:, None, :].astype(u.dtype), window_strides=(1,),
                                    padding=[(GDN_CONV // 2, GDN_CONV // 2)],
                                    dimension_numbers=('NWC', 'WIO', 'NWC'), feature_group_count=ch)


def _gdn_mixer(h_ctx, h_lat, w_in, conv_w, a_log, dt_bias, norm_g, w_out, need_ctx):
    qk_end = 2 * GDN_QK_DIM
    v_end = qk_end + GDN_V_DIM
    z_end = v_end + GDN_V_DIM
    b_end = z_end + 2 * GDN_V_HEADS
    n_ctx = h_ctx.shape[1]

    def prep(h):
        u = h @ w_in
        qkv = jax.nn.silu(_centred_depthwise_conv(u[..., :v_end], conv_w))
        q = _flat(_l2_normalize(_heads(qkv[..., :GDN_QK_DIM], GDN_HEAD_K)))
        k = _flat(_l2_normalize(_heads(qkv[..., GDN_QK_DIM:qk_end], GDN_HEAD_K)))
        v = qkv[..., qk_end:]
        z = u[..., v_end:z_end]
        beta = jax.nn.sigmoid(_heads(u[..., z_end:b_end], GDN_V_HEADS))
        g = -jnp.exp(a_log) * jax.nn.softplus(_heads(u[..., b_end:], GDN_V_HEADS) + dt_bias)
        return q, k, v, z, beta, g

    pc = prep(h_ctx)
    pL = prep(h_lat)
    q, k, v, z, beta, g = (jnp.concatenate([xc, xl], axis=1) for xc, xl in zip(pc, pL))
    B, T = q.shape[0], q.shape[1]
    rep = GDN_V_HEADS // GDN_K_HEADS

    def per_dir_lanes(t):
        return jnp.moveaxis(t, 2, 0)[..., None]

    beta_l = per_dir_lanes(beta)
    g_l = per_dir_lanes(g)
    k_rep = jnp.repeat(_heads(k, GDN_HEAD_K), rep, axis=2)[None]
    lw = _flat(jnp.broadcast_to(g_l, (2, B, T, GDN_V_HEADS, GDN_HEAD_K)))
    b_op = _flat(-(beta_l * jnp.exp(g_l)) * k_rep)
    v_op = _flat(beta_l * _heads(v, GDN_HEAD_V)[None])
    q_op = (q * (GDN_HEAD_K ** -0.5))[None]
    o = _dplr_scan(q_op, lw, k[None], v_op, k[None], b_op, n_ctx=n_ctx,
                   groups_per_step=8, heads_per_group=1, scalar_decay=True)
    o = _heads(o[0] + o[1], GDN_HEAD_V)

    y = _flat(_rms_norm(o, norm_g) * jax.nn.silu(_heads(z, GDN_HEAD_V)))
    if need_ctx:
        out = y @ w_out
        return out[:, :n_ctx], out[:, n_ctx:]
    return None, y[:, n_ctx:] @ w_out


def _ec_moe(h, router, w13, w2):
    B, N, D = h.shape
    E = router.shape[-1]
    cap = EC_FACTOR * N // E
    aff = jax.nn.softmax(h @ router, axis=-1)
    gate, idx = lax.top_k(jnp.swapaxes(aff, 1, 2), cap)
    flat_idx = idx + (jnp.arange(B) * N)[:, None, None]
    hf = h.reshape(B * N, D)
    xin = hf.astype(BF16)[jnp.swapaxes(flat_idx, 0, 1).reshape(E, B * cap)]
    y = _moe_ffn(xin, w13, w2, block_m=1024, block_f=256)
    y = y.reshape(E, B, cap, D) * jnp.swapaxes(gate, 0, 1)[..., None]
    out = jnp.zeros_like(hf).at[jnp.swapaxes(flat_idx, 0, 1).reshape(-1)].add(y.reshape(-1, D))
    return out.reshape(B, N, D)


def kernel(x, c, ctx, c_ctx, mod_w, mod_b, norm_g, rw_mu, rw_w_rkv, rw_w_o, rw_w0, rw_w1, rw_w2, rw_a0, rw_a1, rw_a2, rw_g1, rw_g2, rw_vecs, mla_w_in, mla_qa_g, mla_kva_g, mla_w_qb, mla_w_kvb, mla_qk_g, mla_w_o, gdn_w_in, gdn_conv, gdn_a_log, gdn_dt_bias, gdn_norm_g, gdn_w_out, moe_router, moe_w13, moe_w2):
    depth = mod_w.shape[0]
    for i in range(depth):
        need_ctx = i != depth - 1
        m_lat = jax.nn.silu(c) @ mod_w[i] + mod_b[i]
        m_ctx = jax.nn.silu(c_ctx)[None] @ mod_w[i] + mod_b[i]
        sl = jnp.split(m_lat[:, None, :], 6, axis=-1)
        sc = jnp.split(m_ctx[:, None, :], 6, axis=-1)
        h_lat = _rms_norm(x, norm_g[i, 0]) * (1.0 + sl[1]) + sl[0]
        h_ctx = _rms_norm(ctx, norm_g[i, 0]) * (1.0 + sc[1]) + sc[0]
        kind = i % N_MIXERS
        j = i // N_MIXERS
        if kind == 0:
            y_ctx, y_lat = _rwkv_mixer(h_ctx, h_lat, rw_mu[j], rw_w_rkv[j], rw_w_o[j], rw_w0[j], rw_w1[j],
                                       rw_w2[j], rw_a0[j], rw_a1[j], rw_a2[j], rw_g1[j], rw_g2[j],
                                       rw_vecs[j], need_ctx)
        elif kind == 1:
            y_ctx, y_lat = _mla_mixer(h_ctx, h_lat, mla_w_in[j], mla_qa_g[j], mla_kva_g[j], mla_w_qb[j],
                                      mla_w_kvb[j], mla_qk_g[j], mla_w_o[j], need_ctx)
        else:
            y_ctx, y_lat = _gdn_mixer(h_ctx, h_lat, gdn_w_in[j], gdn_conv[j], gdn_a_log[j], gdn_dt_bias[j],
                                      gdn_norm_g[j], gdn_w_out[j], need_ctx)
        x = x + sl[2] * y_lat
        h = _rms_norm(x, norm_g[i, 1]) * (1.0 + sl[4]) + sl[3]
        x = x + sl[5] * _ec_moe(h, moe_router[i], moe_w13[i], moe_w2[i])
        if need_ctx:
            ctx = ctx + sc[2] * y_ctx
            hc = _rms_norm(ctx, norm_g[i, 1]) * (1.0 + sc[4]) + sc[3]
            ctx = ctx + sc[5] * _ec_moe(hc, moe_router[i], moe_w13[i], moe_w2[i])
    return x
```

```python
import functools

import jax
import jax.numpy as jnp
from jax import lax
from jax.experimental import pallas as pl
from jax.experimental.pallas import tpu as pltpu

F32 = jnp.float32
BF16 = jnp.bfloat16

GRID_W = 64
N_MIXERS = 3
NORM_EPS = 1e-6
RW_HEAD = 64
RW_GN_EPS = 64e-5
MLA_HEADS = 8
MLA_Q_LORA = 384
MLA_KV_LORA = 256
MLA_NOPE = 128
MLA_ROPE = 64
MLA_V = 128
MLA_QK = MLA_NOPE + MLA_ROPE
ROPE_THETA = 10000.0
GDN_K_HEADS = 8
GDN_V_HEADS = 16
GDN_HEAD_K = 128
GDN_HEAD_V = 128
GDN_CONV = 5
GDN_QK_DIM = GDN_K_HEADS * GDN_HEAD_K
GDN_V_DIM = GDN_V_HEADS * GDN_HEAD_V
EC_FACTOR = 2

LANES = 128
SCAN_CHUNK = 64
VMEM_LIMIT_BYTES = 48 * 1024 * 1024


def _bdot(a, b, dims):
    return lax.dot_general(a.astype(BF16), b.astype(BF16), (dims, ((), ())),
                           preferred_element_type=F32)


def _dot_nn(a, b):
    return _bdot(a, b, ((1,), (0,)))


def _dot_nt(a, b):
    return _bdot(a, b, ((1,), (1,)))


def _dot_tn(a, b):
    return _bdot(a, b, ((0,), (0,)))


def _split3(x):
    hi = x.astype(BF16)
    r1 = x - hi.astype(F32)
    mid = r1.astype(BF16)
    lo = (r1 - mid.astype(F32)).astype(BF16)
    return hi, mid, lo


def _exact3(dot, x, fixed, x_first):
    hi, mid, lo = _split3(x)
    if x_first:
        return (dot(hi, fixed) + dot(mid, fixed)) + dot(lo, fixed)
    return (dot(fixed, hi) + dot(fixed, mid)) + dot(fixed, lo)


def _dplr_kernel(*refs, groups, hpl, reps, delta_rule):
    if delta_rule:
        q_ref, g_ref, k_ref, v_ref, beta_ref, o_ref, state_ref = refs
    else:
        q_ref, lw_ref, k_ref, v_ref, a_ref, b_ref, o_ref, state_ref = refs
    d = pl.program_id(2)
    c = pl.program_id(3)
    C = o_ref.shape[-2]
    hC = hpl * C
    G = range(groups)

    @pl.when(c == 0)
    def _():
        state_ref[...] = jnp.zeros_like(state_ref)

    assert C & (C - 1) == 0 and hpl & (hpl - 1) == 0
    log_c = C.bit_length() - 1
    head_w = LANES // hpl
    log_hw = head_w.bit_length() - 1

    def time_masks(n):
        row = lax.broadcasted_iota(jnp.int32, (n, n), 0)
        col = lax.broadcasted_iota(jnp.int32, (n, n), 1)
        lag = ((row & (C - 1)) - (col & (C - 1))) * (1 - 2 * d)
        same = (row >> log_c) == (col >> log_c)
        return same & (lag >= 0), same & (lag > 0), row == col

    incl_c, _, diag_c = time_masks(C)
    incl, strict, diag = time_masks(hC)
    eye = jnp.where(diag, 1.0, 0.0).astype(F32)
    tri = jnp.where(incl_c, 1.0, 0.0).astype(F32)
    lane = lax.broadcasted_iota(jnp.int32, (1, LANES), 1)
    lane_masks = [(lane >> log_hw) == j for j in range(hpl)]

    def stack(x):
        if hpl == 1:
            return x
        return jnp.concatenate([jnp.where(m, x, jnp.zeros_like(x)) for m in lane_masks], axis=0)

    def fold(x):
        out = x[0:C]
        for j in range(1, hpl):
            out = out + x[j * C:(j + 1) * C]
        return out

    def grp(ref, g, rep):
        j = g // rep
        return ref[:, j * LANES:(j + 1) * LANES]

    if delta_rule:
        assert hpl == 1
        rq, rk, rv = reps
        q = [grp(q_ref, g, rq) for g in G]
        k = [grp(k_ref, g, rk) for g in G]
        vin = [grp(v_ref, g, rv) for g in G]
        g_all = g_ref[...]
        beta_all = beta_ref[...]
        cs_all = _exact3(_dot_nn, g_all, tri, x_first=False)
        cs_rows = _exact3(_dot_tn, cs_all, jnp.where(diag_c, 1.0, 0.0).astype(F32), x_first=True)
        tot_all = jnp.sum(g_all, axis=0, keepdims=True)
        ex_all = cs_all - g_all
        e_ex_all = jnp.exp(ex_all)
        e_cs_all = jnp.exp(cs_all)
        e_end_all = jnp.exp(tot_all - cs_all)
        e_tot_all = jnp.exp(tot_all)
        nb_all = -(beta_all * jnp.exp(g_all))

        def col(x, g, n):
            return jnp.broadcast_to(x[:, g:g + 1], (x.shape[0], n))

        a = k
        b = [col(nb_all, g, LANES) * k[g] for g in G]
        v = [col(beta_all, g, LANES) * vin[g] for g in G]
        at = [a[g] * col(e_ex_all, g, LANES) for g in G]
        qt = [q[g] * col(e_cs_all, g, LANES) for g in G]
        w_end = [col(e_end_all, g, LANES) for g in G]
        w_tot = [col(e_tot_all, g, LANES) for g in G]
        v_st = v
        cs_src = [jnp.broadcast_to(cs_rows[g:g + 1, :], (C, C)) for g in G]
        d_strict = [jnp.exp(jnp.where(strict, col(ex_all, g, C) - cs_src[g], -1e30)) for g in G]
        d_incl = [jnp.exp(jnp.where(incl, col(cs_all, g, C) - cs_src[g], -1e30)) for g in G]
        lhs = [jnp.concatenate([a[g], q[g]], axis=0) for g in G]
        xb = [_dot_nt(lhs[g], b[g]) for g in G]
        xk = [_dot_nt(lhs[g], k[g]) for g in G]
        a_ab = [xb[g][:hC] * d_strict[g] for g in G]
        a_qb = [xb[g][hC:] * d_incl[g] for g in G]
        a_ak = [xk[g][:hC] * d_strict[g] for g in G]
        a_qk = [xk[g][hC:] * d_incl[g] for g in G]
    else:
        rq, rlw, rk, rv, ra, rb = reps
        q = [grp(q_ref, g, rq) for g in G]
        lw = [grp(lw_ref, g, rlw) for g in G]
        k = [grp(k_ref, g, rk) for g in G]
        v = [grp(v_ref, g, rv) for g in G]
        a = [grp(a_ref, g, ra) for g in G]
        b = [grp(b_ref, g, rb) for g in G]
        cs = [_exact3(_dot_nn, lw[g], tri, x_first=False) for g in G]
        total = [jnp.sum(lw[g], axis=0, keepdims=True) for g in G]
        w_end = [jnp.exp(total[g] - cs[g]) for g in G]
        w_tot = [jnp.exp(total[g]) for g in G]
        at = [a[g] * jnp.exp(cs[g] - lw[g]) for g in G]
        qt = [q[g] * jnp.exp(cs[g]) for g in G]
        v_st = [stack(v[g]) for g in G]
        w_inv = [jnp.exp(-cs[g]) for g in G]
        lhs = [jnp.concatenate([stack(at[g]), stack(qt[g])], axis=0) for g in G]
        xb = [_dot_nt(lhs[g], stack(b[g] * w_inv[g])) for g in G]
        xk = [_dot_nt(lhs[g], stack(k[g] * w_inv[g])) for g in G]
        a_ab = [jnp.where(strict, xb[g][:hC], 0.0) for g in G]
        a_qb = [jnp.where(incl, xb[g][hC:], 0.0) for g in G]
        a_ak = [jnp.where(strict, xk[g][:hC], 0.0) for g in G]
        a_qk = [jnp.where(incl, xk[g][hC:], 0.0) for g in G]

    s0 = [state_ref[g] for g in G]
    sp = [_dot_nt(jnp.concatenate([at[g], qt[g]], axis=0), s0[g]) for g in G]

    inv = [eye + a_ab[g] for g in G]
    apow = a_ab
    steps = 1
    while steps * 2 < C:
        apow = [_dot_nn(apow[g], apow[g]) for g in G]
        inv = [inv[g] + _dot_nn(inv[g], apow[g]) for g in G]
        steps *= 2

    rhs = [stack(sp[g][:C]) + _dot_nn(a_ak[g], v_st[g]) for g in G]
    u_st = [_dot_nn(inv[g], rhs[g]) for g in G]
    o = [sp[g][C:] + fold(_dot_nn(a_qb[g], u_st[g]) + _dot_nn(a_qk[g], v_st[g])) for g in G]
    for g in G:
        o_ref[:, g * LANES:(g + 1) * LANES] = o[g].astype(o_ref.dtype)

    upd = [_dot_tn(jnp.concatenate([fold(u_st[g]), v[g].astype(F32)], axis=0),
                   jnp.concatenate([b[g] * w_end[g], k[g] * w_end[g]], axis=0)) for g in G]
    if hpl > 1:
        r2 = lax.broadcasted_iota(jnp.int32, (LANES, LANES), 0) >> log_hw
        c2 = lax.broadcasted_iota(jnp.int32, (LANES, LANES), 1) >> log_hw
        upd = [jnp.where(r2 == c2, upd[g], 0.0) for g in G]
    for g in G:
        state_ref[g] = s0[g] * w_tot[g] + upd[g]


def _dplr_scan(ops, *, n_ctx, groups_per_step, heads_per_group, delta_rule):
    v = ops[3]
    B, T, DV = v.shape[1], v.shape[2], v.shape[3]
    C = SCAN_CHUNK
    G = groups_per_step
    assert T % C == 0 and n_ctx % C == 0 and DV % (G * LANES) == 0
    nc = T // C
    nc_ctx = n_ctx // C
    per_head = (1, 4) if delta_rule else ()
    if delta_rule:
        assert DV == G * LANES

    def chunk_index(d, c):
        rev = jnp.where(c < nc_ctx, nc_ctx - 1 - c, nc + nc_ctx - 1 - c)
        return jnp.where(d == 0, c, rev)

    reps, specs = [], []
    for i, x in enumerate(ops):
        per_dir = x.shape[0] == 2
        if i in per_head:
            assert x.shape[3] == LANES and G <= LANES
            specs.append(pl.BlockSpec(
                (None, None, C, LANES),
                functools.partial(
                    lambda bi, hg, d, c, per_dir: (d if per_dir else 0, bi, chunk_index(d, c), 0),
                    per_dir=per_dir)))
            continue
        assert DV % x.shape[3] == 0
        rep = DV // x.shape[3]
        assert G % rep == 0
        reps.append(rep)
        specs.append(pl.BlockSpec(
            (None, None, C, (G // rep) * LANES),
            functools.partial(
                lambda bi, hg, d, c, per_dir: (d if per_dir else 0, bi, chunk_index(d, c), hg),
                per_dir=per_dir)))
    out_spec = pl.BlockSpec((None, None, C, G * LANES),
                            lambda bi, hg, d, c: (d, bi, chunk_index(d, c), hg))
    return pl.pallas_call(
        functools.partial(_dplr_kernel, groups=G, hpl=heads_per_group, reps=tuple(reps),
                          delta_rule=delta_rule),
        grid=(B, DV // (G * LANES), 2, nc),
        in_specs=specs,
        out_specs=out_spec,
        out_shape=jax.ShapeDtypeStruct((2, B, T, DV), F32),
        scratch_shapes=[pltpu.VMEM((G, LANES, LANES), F32)],
        compiler_params=pltpu.CompilerParams(
            dimension_semantics=("parallel", "parallel", "arbitrary", "arbitrary"),
            vmem_limit_bytes=VMEM_LIMIT_BYTES),
        name="dplr_scan",
    )(*ops)


def _attention_kernel(q_ref, k_ref, v_ref, o_ref, *, scale, tk):
    q = q_ref[...]
    dv = o_ref.shape[-1]
    c = scale * 1.4426950408889634
    n_chunks = k_ref.shape[0] // tk

    def scores(j):
        return _dot_nt(q, k_ref[j * tk:(j + 1) * tk, :])

    s = scores(0)
    m = acc = None
    for j in range(n_chunks):
        s_next = scores(j + 1) if j + 1 < n_chunks else None
        m_chunk = jnp.max(s, axis=-1, keepdims=True)
        m_new = m_chunk if m is None else jnp.maximum(m, m_chunk)
        p = jnp.exp2((s - m_new) * c)
        pv = _dot_nn(p, v_ref[j * tk:(j + 1) * tk, :])
        acc = pv if acc is None else jnp.exp2((m - m_new) * c) * acc + pv
        m, s = m_new, s_next
    o_ref[...] = (acc[:, :dv] / acc[:, dv:dv + 1]).astype(o_ref.dtype)


def _largest_divisor(n, candidates):
    return next(c for c in candidates if n % c == 0)


def _attention(q, k, v, *, scale, block_q):
    B, H, Tq, dk = q.shape
    Tk, dv = k.shape[2], v.shape[3]
    tq = min(block_q, Tq)
    assert Tq % tq == 0 and dv % LANES == 0
    tk = _largest_divisor(Tk, (1408, 768, 512, 256, 128, 64, 32, 16))
    v = jnp.concatenate([v, jnp.ones((B, H, Tk, LANES), v.dtype)], axis=-1)
    return pl.pallas_call(
        functools.partial(_attention_kernel, scale=scale, tk=tk),
        grid=(B, H, Tq // tq),
        in_specs=[
            pl.BlockSpec((None, None, tq, dk), lambda b, h, i: (b, h, i, 0)),
            pl.BlockSpec((None, None, Tk, dk), lambda b, h, i: (b, h, 0, 0)),
            pl.BlockSpec((None, None, Tk, dv + LANES), lambda b, h, i: (b, h, 0, 0)),
        ],
        out_specs=pl.BlockSpec((None, None, tq, dv), lambda b, h, i: (b, h, i, 0)),
        out_shape=jax.ShapeDtypeStruct((B, H, Tq, dv), F32),
        compiler_params=pltpu.CompilerParams(
            dimension_semantics=("parallel", "parallel", "arbitrary"),
            vmem_limit_bytes=VMEM_LIMIT_BYTES),
        name="mla_attention",
    )(q, k, v)


def _moe_ffn_kernel(x_ref, wg_ref, wu_ref, wd_ref, gate_ref, mod_ctx_ref, mod_lat_ref, o_ref, *,
                    ctx_slots):
    f = pl.program_id(2)
    x = x_ref[...]
    g = _dot_nn(x, wg_ref[...])
    u = _dot_nn(x, wu_ref[...])
    act = (g * jax.nn.sigmoid(g)) * u
    y = _dot_nn(act, wd_ref[...])

    @pl.when(f == 0)
    def _():
        o_ref[...] = y

    @pl.when(f != 0)
    def _():
        o_ref[...] += y

    @pl.when(f == pl.num_programs(2) - 1)
    def _():
        rows = lax.broadcasted_iota(jnp.int32, (o_ref.shape[0], 1), 0)
        mod = jnp.where(rows < ctx_slots, mod_ctx_ref[...], mod_lat_ref[...])
        o_ref[...] = o_ref[...] * gate_ref[...] * mod


def _moe_ffn(x, w13, w2, gate, mod_ctx, mod_lat, *, ctx_slots, block_f):
    E, M, D = x.shape
    B = mod_lat.shape[0]
    Fh = w2.shape[1]
    tm = M // B
    tf = min(block_f, Fh)
    assert M % B == 0 and Fh % tf == 0
    nf = Fh // tf
    return pl.pallas_call(
        functools.partial(_moe_ffn_kernel, ctx_slots=ctx_slots),
        grid=(E, B, nf),
        in_specs=[
            pl.BlockSpec((None, tm, D), lambda e, m, f: (e, m, 0)),
            pl.BlockSpec((None, D, tf), lambda e, m, f: (e, 0, f)),
            pl.BlockSpec((None, D, tf), lambda e, m, f: (e, 0, nf + f)),
            pl.BlockSpec((None, tf, D), lambda e, m, f: (e, f, 0)),
            pl.BlockSpec((None, tm, 1), lambda e, m, f: (e, m, 0)),
            pl.BlockSpec((1, D), lambda e, m, f: (0, 0)),
            pl.BlockSpec((None, 1, D), lambda e, m, f: (m, 0, 0)),
        ],
        out_specs=pl.BlockSpec((None, tm, D), lambda e, m, f: (e, m, 0)),
        out_shape=jax.ShapeDtypeStruct((E, M, D), F32),
        compiler_params=pltpu.CompilerParams(
            dimension_semantics=("parallel", "parallel", "arbitrary"),
            vmem_limit_bytes=VMEM_LIMIT_BYTES),
        name="moe_ffn",
    )(x, w13, w13, w2, gate, mod_ctx, mod_lat)


def _rms_norm(x, g, eps=NORM_EPS):
    return x * lax.rsqrt(jnp.mean(x * x, -1, keepdims=True) + eps) * g


def _l2_normalize(x, eps=1e-6):
    return x * lax.rsqrt(jnp.sum(x * x, -1, keepdims=True) + eps)


def _heads(t, dim):
    return t.reshape(t.shape[:-1] + (t.shape[-1] // dim, dim))


def _flat(t):
    return t.reshape(t.shape[:-2] + (t.shape[-2] * t.shape[-1],))


def _segment_shifted(h, offset, n_ctx):
    if offset == 0:
        return h
    T = h.shape[1]
    if offset > 0:
        moved = jnp.pad(h[:, offset:], ((0, 0), (0, offset), (0, 0)))
    else:
        moved = jnp.pad(h[:, :offset], ((0, 0), (-offset, 0), (0, 0)))
    t = jnp.arange(T)
    same_segment = ((t + offset) < n_ctx) == (t < n_ctx)
    return jnp.where(same_segment[None, :, None], moved, 0.0)


def _rwkv_mixer(h, n_ctx, mu, w_rkv, w_o, w0, w1, w2, a0, a1, a2, g1, g2, vecs):
    k_k, k_a, r_k, ln_w, ln_b = (vecs[j] for j in range(5))
    xx = 0.5 * (_segment_shifted(h, -1, n_ctx) + _segment_shifted(h, 1, n_ctx)) - h
    xr, xw, xk, xv, xa, xg = ((h + xx * mu[j]).astype(BF16) for j in range(6))
    r = jnp.dot(xr, w_rkv[0].astype(BF16), preferred_element_type=F32)
    k = jnp.dot(xk, w_rkv[1].astype(BF16), preferred_element_type=F32)
    v = jnp.dot(xv, w_rkv[2].astype(BF16), preferred_element_type=F32)
    gate = jax.nn.sigmoid(jnp.dot(xg, g1.astype(BF16), preferred_element_type=F32)) @ g2
    kk = _flat(_l2_normalize(_heads(k * k_k, RW_HEAD)))
    w_lora = jnp.tanh(jnp.einsum('btd,kde->kbte', xw, w1.astype(BF16), preferred_element_type=F32))
    w_log = -jax.nn.softplus(-(w0[:, None, None, :] + jnp.einsum('kbte,ked->kbtd', w_lora, w2))) - 0.5
    lw = -jnp.exp(w_log)
    a_lora = jnp.einsum('btd,kde->kbte', xa, a1.astype(BF16), preferred_element_type=F32)
    a = jax.nn.sigmoid(a0[:, None, None, :] + jnp.einsum('kbte,ked->kbtd', a_lora, a2))
    kd = k[None] * (1.0 + (a - 1.0) * k_a)
    bb = kk[None] * a
    o = _dplr_scan((r.astype(BF16)[None], lw, kd.astype(BF16), v.astype(BF16)[None],
                    (-kk).astype(BF16)[None], bb.astype(BF16)),
                   n_ctx=n_ctx, groups_per_step=r.shape[-1] // LANES,
                   heads_per_group=LANES // RW_HEAD, delta_rule=False)
    o_sum = _heads(o[0] + o[1], RW_HEAD)
    bonus = jnp.sum(_heads(r * (kd[0] + kd[1]) * r_k, RW_HEAD), -1, keepdims=True) * _heads(v, RW_HEAD)
    mean = jnp.mean(o_sum, -1, keepdims=True)
    var = jnp.mean(jnp.square(o_sum - mean), -1, keepdims=True)
    y = (o_sum - mean) * lax.rsqrt(var + RW_GN_EPS)
    y = _flat(y * _heads(ln_w, RW_HEAD) + _heads(ln_b, RW_HEAD) + bonus)
    return jnp.dot((y * gate).astype(BF16), w_o.astype(BF16), preferred_element_type=F32)


def _axial_rope(t, n_ctx):
    T = t.shape[1]
    pos = jnp.maximum(jnp.arange(T) - n_ctx, 0)
    is_lat = (jnp.arange(T) >= n_ctx).astype(F32)
    row_id = (pos // GRID_W).astype(F32) * is_lat
    col_id = (pos % GRID_W).astype(F32) * is_lat
    nf = MLA_ROPE // 4
    inv = ROPE_THETA ** (-jnp.arange(nf, dtype=F32) / nf)

    def rot(seg, p):
        ang = p[:, None] * inv
        cos = jnp.cos(ang)[:, None, :]
        sin = jnp.sin(ang)[:, None, :]
        s1, s2 = seg[..., :nf], seg[..., nf:]
        return jnp.concatenate([s1 * cos - s2 * sin, s2 * cos + s1 * sin], -1)

    half = MLA_ROPE // 2
    return jnp.concatenate([rot(t[..., :half], row_id), rot(t[..., half:], col_id)], -1)


def _mla_mixer(h, n_ctx, w_in, qa_g, kva_g, w_qb, w_kvb, qk_g, w_o):
    B, T, _ = h.shape
    u = jnp.dot(h.astype(BF16), w_in.astype(BF16), preferred_element_type=F32)
    cq = _rms_norm(u[..., :MLA_Q_LORA], qa_g)
    ckv = _rms_norm(u[..., MLA_Q_LORA:MLA_Q_LORA + MLA_KV_LORA], kva_g)
    k_pe = u[..., MLA_Q_LORA + MLA_KV_LORA:]
    q = (cq @ w_qb).reshape(B, T, MLA_HEADS, MLA_QK)
    kv = (ckv @ w_kvb).reshape(B, T, MLA_HEADS, MLA_NOPE + MLA_V)
    k_nope, v = kv[..., :MLA_NOPE], kv[..., MLA_NOPE:]
    k = jnp.concatenate([k_nope, jnp.broadcast_to(k_pe[:, :, None, :], (B, T, MLA_HEADS, MLA_ROPE))], -1)
    q = _rms_norm(q, qk_g[0])
    k = _rms_norm(k, qk_g[1])
    q = jnp.concatenate([q[..., :MLA_NOPE], _axial_rope(q[..., MLA_NOPE:], n_ctx)], -1)
    k = jnp.concatenate([k[..., :MLA_NOPE], _axial_rope(k[..., MLA_NOPE:], n_ctx)], -1)

    def hm(t):
        return jnp.swapaxes(t, 1, 2).astype(BF16)

    scale = MLA_QK ** -0.5
    qh, kh, vh = hm(q), hm(k), hm(v)
    o_lat = _attention(qh[:, :, n_ctx:], kh, vh, scale=scale, block_q=512)
    o_ctx = _attention(qh[:, :, :n_ctx], kh[:, :, :n_ctx], vh[:, :, :n_ctx], scale=scale, block_q=512)
    o = jnp.concatenate([o_ctx, o_lat], axis=2)
    o = jnp.swapaxes(o, 1, 2).reshape(B, T, MLA_HEADS * MLA_V)
    return jnp.dot(o.astype(BF16), w_o.astype(BF16), preferred_element_type=F32)


def _gdn_mixer(h, n_ctx, w_in, conv_w, a_log, dt_bias, norm_g, w_out):
    qk_end = 2 * GDN_QK_DIM
    v_end = qk_end + GDN_V_DIM
    z_end = v_end + GDN_V_DIM
    b_end = z_end + 2 * GDN_V_HEADS
    u = jnp.dot(h.astype(BF16), w_in.astype(BF16), preferred_element_type=F32)
    uc = u[..., :v_end]
    half = GDN_CONV // 2
    conv = sum(_segment_shifted(uc, j - half, n_ctx) * conv_w[j] for j in range(GDN_CONV))
    qkv = jax.nn.silu(conv)
    q = _flat(_l2_normalize(_heads(qkv[..., :GDN_QK_DIM], GDN_HEAD_K)))
    k = _flat(_l2_normalize(_heads(qkv[..., GDN_QK_DIM:qk_end], GDN_HEAD_K)))
    v = qkv[..., qk_end:]
    z = u[..., v_end:z_end]
    beta = jax.nn.sigmoid(_heads(u[..., z_end:b_end], GDN_V_HEADS))
    g = -jnp.exp(a_log) * jax.nn.softplus(_heads(u[..., b_end:], GDN_V_HEADS) + dt_bias)
    q_op = (q * (GDN_HEAD_K ** -0.5)).astype(BF16)[None]

    def per_head_lanes(t):
        return jnp.pad(jnp.moveaxis(t, 2, 0), ((0, 0), (0, 0), (0, 0), (0, LANES - GDN_V_HEADS)))

    o = _dplr_scan((q_op, per_head_lanes(g), k.astype(BF16)[None], v.astype(BF16)[None],
                    per_head_lanes(beta)),
                   n_ctx=n_ctx, groups_per_step=GDN_V_HEADS, heads_per_group=1, delta_rule=True)
    o = _heads(o[0] + o[1], GDN_HEAD_V)
    y = _flat(_rms_norm(o, norm_g) * jax.nn.silu(_heads(z, GDN_HEAD_V)))
    return jnp.dot(y.astype(BF16), w_out.astype(BF16), preferred_element_type=F32)


def _ec_moe_add(xa, h, n_ctx, with_ctx, router, w13, w2, mod_ctx, mod_lat):
    B, T, D = h.shape
    E = router.shape[-1]
    aff = jnp.swapaxes(jax.nn.softmax(h @ router, axis=-1), 1, 2)
    n_lat = T - n_ctx
    gate, idx = lax.top_k(aff[:, :, n_ctx:], EC_FACTOR * n_lat // E)
    idx = idx + n_ctx
    ctx_slots = 0
    if with_ctx:
        ctx_slots = EC_FACTOR * n_ctx // E
        gate_c, idx_c = lax.top_k(aff[:, :, :n_ctx], ctx_slots)
        gate = jnp.concatenate([gate_c, gate], axis=-1)
        idx = jnp.concatenate([idx_c, idx], axis=-1)
    slots = idx.shape[-1]
    flat_idx = jnp.swapaxes(idx + (jnp.arange(B) * T)[:, None, None], 0, 1).reshape(E, B * slots)
    gate = jnp.swapaxes(gate, 0, 1).reshape(E, B * slots, 1)
    hf = h.astype(BF16).reshape(B * T, D)
    y = _moe_ffn(hf[flat_idx], w13, w2, gate, mod_ctx, mod_lat, ctx_slots=ctx_slots, block_f=256)
    out = xa.reshape(B * T, D).at[flat_idx.reshape(-1)].add(y.reshape(-1, D))
    return out.reshape(B, T, D)


def kernel(x, c, ctx, c_ctx, mod_w, mod_b, norm_g, rw_mu, rw_w_rkv, rw_w_o, rw_w0, rw_w1, rw_w2, rw_a0, rw_a1, rw_a2, rw_g1, rw_g2, rw_vecs, mla_w_in, mla_qa_g, mla_kva_g, mla_w_qb, mla_w_kvb, mla_qk_g, mla_w_o, gdn_w_in, gdn_conv, gdn_a_log, gdn_dt_bias, gdn_norm_g, gdn_w_out, moe_router, moe_w13, moe_w2):
    depth = mod_w.shape[0]
    n_ctx = ctx.shape[1]
    xa = jnp.concatenate([ctx, x], axis=1)
    is_ctx = (jnp.arange(xa.shape[1]) < n_ctx)[None, :, None]
    for i in range(depth):
        last = i == depth - 1
        m_lat = jax.nn.silu(c) @ mod_w[i] + mod_b[i]
        m_ctx = jax.nn.silu(c_ctx)[None] @ mod_w[i] + mod_b[i]
        sl = jnp.split(m_lat[:, None, :], 6, axis=-1)
        sc = jnp.split(m_ctx[:, None, :], 6, axis=-1)
        mod = [jnp.where(is_ctx, sc[j], sl[j]) for j in range(5)]
        h = _rms_norm(xa, norm_g[i, 0]) * (1.0 + mod[1]) + mod[0]
        kind = i % N_MIXERS
        j = i // N_MIXERS
        if kind == 0:
            y = _rwkv_mixer(h, n_ctx, rw_mu[j], rw_w_rkv[j], rw_w_o[j], rw_w0[j], rw_w1[j], rw_w2[j],
                            rw_a0[j], rw_a1[j], rw_a2[j], rw_g1[j], rw_g2[j], rw_vecs[j])
        elif kind == 1:
            y = _mla_mixer(h, n_ctx, mla_w_in[j], mla_qa_g[j], mla_kva_g[j], mla_w_qb[j],
                           mla_w_kvb[j], mla_qk_g[j], mla_w_o[j])
        else:
            y = _gdn_mixer(h, n_ctx, gdn_w_in[j], gdn_conv[j], gdn_a_log[j], gdn_dt_bias[j],
                           gdn_norm_g[j], gdn_w_out[j])
        xa = xa + mod[2] * y
        h2 = _rms_norm(xa, norm_g[i, 1]) * (1.0 + mod[4]) + mod[3]
        xa = _ec_moe_add(xa, h2, n_ctx, not last, moe_router[i], moe_w13[i], moe_w2[i], sc[5][0], sl[5])
    return xa[:, n_ctx:]
```

```python
import functools

import jax
import jax.numpy as jnp
from jax import lax
from jax.experimental import pallas as pl
from jax.experimental.pallas import tpu as pltpu

F32 = jnp.float32
BF16 = jnp.bfloat16

GRID_W = 64
N_MIXERS = 3
NORM_EPS = 1e-6
L2_EPS = 1e-6
RW_HEAD = 64
RW_GN_EPS = 64e-5
MLA_HEADS = 8
MLA_Q_LORA = 384
MLA_KV_LORA = 256
MLA_NOPE = 128
MLA_ROPE = 64
MLA_V = 128
MLA_QK = MLA_NOPE + MLA_ROPE
ROPE_THETA = 10000.0
GDN_K_HEADS = 8
GDN_V_HEADS = 16
GDN_HEAD_K = 128
GDN_HEAD_V = 128
GDN_CONV = 5
GDN_QK_DIM = GDN_K_HEADS * GDN_HEAD_K
GDN_V_DIM = GDN_V_HEADS * GDN_HEAD_V
EC_FACTOR = 2

LANES = 128
SCAN_CHUNK = 64
VMEM_LIMIT_BYTES = 48 * 1024 * 1024


def _bdot(a, b, dims):
    return lax.dot_general(a.astype(BF16), b.astype(BF16), (dims, ((), ())),
                           preferred_element_type=F32)


def _dot_nn(a, b):
    return _bdot(a, b, ((1,), (0,)))


def _dot_nt(a, b):
    return _bdot(a, b, ((1,), (1,)))


def _dot_tn(a, b):
    return _bdot(a, b, ((0,), (0,)))


def _split3(x):
    hi = x.astype(BF16)
    r1 = x - hi.astype(F32)
    mid = r1.astype(BF16)
    lo = (r1 - mid.astype(F32)).astype(BF16)
    return hi, mid, lo


def _exact3(dot, x, fixed, x_first):
    hi, mid, lo = _split3(x)
    if x_first:
        return (dot(hi, fixed) + dot(mid, fixed)) + dot(lo, fixed)
    return (dot(fixed, hi) + dot(fixed, mid)) + dot(fixed, lo)


def _dplr_kernel(*refs, groups, hpl, reps, delta_rule):
    if delta_rule:
        q_ref, g_ref, k_ref, v_ref, beta_ref, o_ref, state_ref = refs
    else:
        q_ref, lw_ref, k_ref, v_ref, a_ref, b_ref, o_ref, state_ref = refs
    d = pl.program_id(2)
    c = pl.program_id(3)
    C = o_ref.shape[-2]
    hC = hpl * C
    G = range(groups)

    @pl.when(c == 0)
    def _():
        state_ref[...] = jnp.zeros_like(state_ref)

    assert C & (C - 1) == 0 and hpl & (hpl - 1) == 0
    log_c = C.bit_length() - 1
    head_w = LANES // hpl
    log_hw = head_w.bit_length() - 1

    def time_masks(n):
        row = lax.broadcasted_iota(jnp.int32, (n, n), 0)
        col = lax.broadcasted_iota(jnp.int32, (n, n), 1)
        lag = ((row & (C - 1)) - (col & (C - 1))) * (1 - 2 * d)
        same = (row >> log_c) == (col >> log_c)
        return same & (lag >= 0), same & (lag > 0), row == col

    incl_c, _, diag_c = time_masks(C)
    incl, strict, diag = time_masks(hC)
    eye = jnp.where(diag, 1.0, 0.0).astype(F32)
    tri = jnp.where(incl_c, 1.0, 0.0).astype(F32)
    lane = lax.broadcasted_iota(jnp.int32, (1, LANES), 1)
    lane_masks = [(lane >> log_hw) == j for j in range(hpl)]

    def stack(x):
        if hpl == 1:
            return x
        return jnp.concatenate([jnp.where(m, x, jnp.zeros_like(x)) for m in lane_masks], axis=0)

    def fold(x):
        out = x[0:C]
        for j in range(1, hpl):
            out = out + x[j * C:(j + 1) * C]
        return out

    def grp(ref, g, rep):
        j = g // rep
        return ref[:, j * LANES:(j + 1) * LANES]

    if delta_rule:
        assert hpl == 1
        rq, rk, rv = reps
        q = [grp(q_ref, g, rq) for g in G]
        k = [grp(k_ref, g, rk) for g in G]
        vin = [grp(v_ref, g, rv) for g in G]
        g_all = g_ref[...]
        beta_all = beta_ref[...]
        cs_all = _exact3(_dot_nn, g_all, tri, x_first=False)
        cs_rows = _exact3(_dot_tn, cs_all, jnp.where(diag_c, 1.0, 0.0).astype(F32), x_first=True)
        tot_all = jnp.sum(g_all, axis=0, keepdims=True)
        ex_all = cs_all - g_all
        e_ex_all = jnp.exp(ex_all)
        e_cs_all = jnp.exp(cs_all)
        e_end_all = jnp.exp(tot_all - cs_all)
        e_tot_all = jnp.exp(tot_all)
        nb_all = -(beta_all * jnp.exp(g_all))

        def col(x, g, n):
            return jnp.broadcast_to(x[:, g:g + 1], (x.shape[0], n))

        a = k
        b = [col(nb_all, g, LANES) * k[g] for g in G]
        v = [col(beta_all, g, LANES) * vin[g] for g in G]
        at = [a[g] * col(e_ex_all, g, LANES) for g in G]
        qt = [q[g] * col(e_cs_all, g, LANES) for g in G]
        w_end = [col(e_end_all, g, LANES) for g in G]
        w_tot = [col(e_tot_all, g, LANES) for g in G]
        v_st = v
        cs_src = [jnp.broadcast_to(cs_rows[g:g + 1, :], (C, C)) for g in G]
        d_strict = [jnp.exp(jnp.where(strict, col(ex_all, g, C) - cs_src[g], -1e30)) for g in G]
        d_incl = [jnp.exp(jnp.where(incl, col(cs_all, g, C) - cs_src[g], -1e30)) for g in G]
        lhs = [jnp.concatenate([a[g], q[g]], axis=0) for g in G]
        xb = [_dot_nt(lhs[g], b[g]) for g in G]
        xk = [_dot_nt(lhs[g], k[g]) for g in G]
        a_ab = [xb[g][:hC] * d_strict[g] for g in G]
        a_qb = [xb[g][hC:] * d_incl[g] for g in G]
        a_ak = [xk[g][:hC] * d_strict[g] for g in G]
        a_qk = [xk[g][hC:] * d_incl[g] for g in G]
    else:
        rq, rlw, rk, rv, ra, rb = reps
        q = [grp(q_ref, g, rq) for g in G]
        lw = [grp(lw_ref, g, rlw) for g in G]
        k = [grp(k_ref, g, rk) for g in G]
        v = [grp(v_ref, g, rv) for g in G]
        a = [grp(a_ref, g, ra) for g in G]
        b = [grp(b_ref, g, rb) for g in G]
        cs = [_exact3(_dot_nn, lw[g], tri, x_first=False) for g in G]
        total = [jnp.sum(lw[g], axis=0, keepdims=True) for g in G]
        w_end = [jnp.exp(total[g] - cs[g]) for g in G]
        w_tot = [jnp.exp(total[g]) for g in G]
        at = [a[g] * jnp.exp(cs[g] - lw[g]) for g in G]
        qt = [q[g] * jnp.exp(cs[g]) for g in G]
        v_st = [stack(v[g]) for g in G]
        w_inv = [jnp.exp(-cs[g]) for g in G]
        lhs = [jnp.concatenate([stack(at[g]), stack(qt[g])], axis=0) for g in G]
        xb = [_dot_nt(lhs[g], stack(b[g] * w_inv[g])) for g in G]
        xk = [_dot_nt(lhs[g], stack(k[g] * w_inv[g])) for g in G]
        a_ab = [jnp.where(strict, xb[g][:hC], 0.0) for g in G]
        a_qb = [jnp.where(incl, xb[g][hC:], 0.0) for g in G]
        a_ak = [jnp.where(strict, xk[g][:hC], 0.0) for g in G]
        a_qk = [jnp.where(incl, xk[g][hC:], 0.0) for g in G]

    s0 = [state_ref[g] for g in G]
    sp = [_dot_nt(jnp.concatenate([at[g], qt[g]], axis=0), s0[g]) for g in G]

    inv = [eye + a_ab[g] for g in G]
    apow = a_ab
    steps = 1
    while steps * 2 < C:
        apow = [_dot_nn(apow[g], apow[g]) for g in G]
        inv = [inv[g] + _dot_nn(inv[g], apow[g]) for g in G]
        steps *= 2

    rhs = [stack(sp[g][:C]) + _dot_nn(a_ak[g], v_st[g]) for g in G]
    u_st = [_dot_nn(inv[g], rhs[g]) for g in G]
    o = [sp[g][C:] + fold(_dot_nn(a_qb[g], u_st[g]) + _dot_nn(a_qk[g], v_st[g])) for g in G]
    for g in G:
        o_ref[:, g * LANES:(g + 1) * LANES] = o[g].astype(o_ref.dtype)

    upd = [_dot_tn(jnp.concatenate([fold(u_st[g]), v[g].astype(F32)], axis=0),
                   jnp.concatenate([b[g] * w_end[g], k[g] * w_end[g]], axis=0)) for g in G]
    if hpl > 1:
        r2 = lax.broadcasted_iota(jnp.int32, (LANES, LANES), 0) >> log_hw
        c2 = lax.broadcasted_iota(jnp.int32, (LANES, LANES), 1) >> log_hw
        upd = [jnp.where(r2 == c2, upd[g], 0.0) for g in G]
    for g in G:
        state_ref[g] = s0[g] * w_tot[g] + upd[g]


def _dplr_scan(ops, *, n_ctx, groups_per_step, heads_per_group, delta_rule):
    v = ops[3]
    B, T, DV = v.shape[1], v.shape[2], v.shape[3]
    C = SCAN_CHUNK
    G = groups_per_step
    assert T % C == 0 and n_ctx % C == 0 and DV % (G * LANES) == 0
    nc = T // C
    nc_ctx = n_ctx // C
    per_head = (1, 4) if delta_rule else ()
    if delta_rule:
        assert DV == G * LANES

    def chunk_index(d, c):
        rev = jnp.where(c < nc_ctx, nc_ctx - 1 - c, nc + nc_ctx - 1 - c)
        return jnp.where(d == 0, c, rev)

    reps, specs = [], []
    for i, x in enumerate(ops):
        per_dir = x.shape[0] == 2
        if i in per_head:
            assert x.shape[3] == LANES and G <= LANES
            specs.append(pl.BlockSpec(
                (None, None, C, LANES),
                functools.partial(
                    lambda bi, hg, d, c, per_dir: (d if per_dir else 0, bi, chunk_index(d, c), 0),
                    per_dir=per_dir)))
            continue
        assert DV % x.shape[3] == 0
        rep = DV // x.shape[3]
        assert G % rep == 0
        reps.append(rep)
        specs.append(pl.BlockSpec(
            (None, None, C, (G // rep) * LANES),
            functools.partial(
                lambda bi, hg, d, c, per_dir: (d if per_dir else 0, bi, chunk_index(d, c), hg),
                per_dir=per_dir)))
    out_spec = pl.BlockSpec((None, None, C, G * LANES),
                            lambda bi, hg, d, c: (d, bi, chunk_index(d, c), hg))
    return pl.pallas_call(
        functools.partial(_dplr_kernel, groups=G, hpl=heads_per_group, reps=tuple(reps),
                          delta_rule=delta_rule),
        grid=(B, DV // (G * LANES), 2, nc),
        in_specs=specs,
        out_specs=out_spec,
        out_shape=jax.ShapeDtypeStruct((2, B, T, DV), F32),
        scratch_shapes=[pltpu.VMEM((G, LANES, LANES), F32)],
        compiler_params=pltpu.CompilerParams(
            dimension_semantics=("parallel", "parallel", "arbitrary", "arbitrary"),
            vmem_limit_bytes=VMEM_LIMIT_BYTES),
        name="dplr_scan",
    )(*ops)


def _attention_kernel(q_ref, k_ref, v_ref, o_ref, *, scale, tk):
    q = q_ref[...]
    dv = o_ref.shape[-1]
    c = scale * 1.4426950408889634
    n_chunks = k_ref.shape[0] // tk

    def scores(j):
        return _dot_nt(q, k_ref[j * tk:(j + 1) * tk, :])

    s = scores(0)
    m = acc = None
    for j in range(n_chunks):
        s_next = scores(j + 1) if j + 1 < n_chunks else None
        m_chunk = jnp.max(s, axis=-1, keepdims=True)
        m_new = m_chunk if m is None else jnp.maximum(m, m_chunk)
        p = jnp.exp2((s - m_new) * c)
        pv = _dot_nn(p, v_ref[j * tk:(j + 1) * tk, :])
        acc = pv if acc is None else jnp.exp2((m - m_new) * c) * acc + pv
        m, s = m_new, s_next
    o_ref[...] = (acc[:, :dv] / acc[:, dv:dv + 1]).astype(o_ref.dtype)


def _largest_divisor(n, candidates):
    return next(c for c in candidates if n % c == 0)


def _attention(q, k, v, *, scale, block_q):
    B, H, Tq, dk = q.shape
    Tk, dv = k.shape[2], v.shape[3]
    tq = min(block_q, Tq)
    assert Tq % tq == 0 and dv % LANES == 0
    tk = _largest_divisor(Tk, (1408, 768, 512, 256, 128, 64, 32, 16))
    v = jnp.concatenate([v, jnp.ones((B, H, Tk, LANES), v.dtype)], axis=-1)
    return pl.pallas_call(
        functools.partial(_attention_kernel, scale=scale, tk=tk),
        grid=(B, H, Tq // tq),
        in_specs=[
            pl.BlockSpec((None, None, tq, dk), lambda b, h, i: (b, h, i, 0)),
            pl.BlockSpec((None, None, Tk, dk), lambda b, h, i: (b, h, 0, 0)),
            pl.BlockSpec((None, None, Tk, dv + LANES), lambda b, h, i: (b, h, 0, 0)),
        ],
        out_specs=pl.BlockSpec((None, None, tq, dv), lambda b, h, i: (b, h, i, 0)),
        out_shape=jax.ShapeDtypeStruct((B, H, Tq, dv), F32),
        compiler_params=pltpu.CompilerParams(
            dimension_semantics=("parallel", "parallel", "arbitrary"),
            vmem_limit_bytes=VMEM_LIMIT_BYTES),
        name="mla_attention",
    )(q, k, v)


def _moe_ffn_kernel(x_ref, wg_ref, wu_ref, wd_ref, gate_ref, mod_ctx_ref, mod_lat_ref, o_ref, *,
                    ctx_slots):
    f = pl.program_id(2)

    @pl.when(f == 0)
    def _():
        o_ref[...] = jnp.zeros_like(o_ref)

    x = x_ref[...]
    g = _dot_nn(x, wg_ref[...])
    u = _dot_nn(x, wu_ref[...])
    act = (g * jax.nn.sigmoid(g)) * u
    o_ref[...] += _dot_nn(act, wd_ref[...])

    @pl.when(f == pl.num_programs(2) - 1)
    def _():
        rows = lax.broadcasted_iota(jnp.int32, (o_ref.shape[0], 1), 0)
        mod = jnp.where(rows < ctx_slots, mod_ctx_ref[...], mod_lat_ref[...])
        o_ref[...] = o_ref[...] * gate_ref[...] * mod


def _moe_ffn(x, w13, w2, layer, gate, mod_ctx, mod_lat, *, ctx_slots, block_f):
    E, M, D = x.shape
    B = mod_lat.shape[0]
    Fh = w2.shape[2]
    tm = M // B
    tf = min(block_f, Fh)
    assert M % B == 0 and Fh % tf == 0
    nf = Fh // tf
    return pl.pallas_call(
        functools.partial(_moe_ffn_kernel, ctx_slots=ctx_slots),
        grid=(E, B, nf),
        in_specs=[
            pl.BlockSpec((None, tm, D), lambda e, m, f: (e, m, 0)),
            pl.BlockSpec((None, None, D, tf), lambda e, m, f: (layer, e, 0, f)),
            pl.BlockSpec((None, None, D, tf), lambda e, m, f: (layer, e, 0, nf + f)),
            pl.BlockSpec((None, None, tf, D), lambda e, m, f: (layer, e, f, 0)),
            pl.BlockSpec((None, tm, 1), lambda e, m, f: (e, m, 0)),
            pl.BlockSpec((1, D), lambda e, m, f: (0, 0)),
            pl.BlockSpec((None, 1, D), lambda e, m, f: (m, 0, 0)),
        ],
        out_specs=pl.BlockSpec((None, tm, D), lambda e, m, f: (e, m, 0)),
        out_shape=jax.ShapeDtypeStruct((E, M, D), F32),
        compiler_params=pltpu.CompilerParams(
            dimension_semantics=("parallel", "parallel", "arbitrary"),
            vmem_limit_bytes=VMEM_LIMIT_BYTES),
        name="moe_ffn",
    )(x, w13, w13, w2, gate, mod_ctx, mod_lat)


HALO = 8


def _gdn_prep_kernel(prev_ref, cur_ref, next_ref, w_ref, q_ref, k_ref, v_ref, *, n_ctx, n_rows,
                     q_scale):
    tm = cur_ref.shape[0]
    taps = w_ref.shape[0]
    ext = jnp.concatenate([prev_ref[...], cur_ref[...], next_ref[...]], axis=0)
    t = pl.program_id(1) * tm + lax.broadcasted_iota(jnp.int32, (tm, 1), 0)
    acc = None
    for j in range(taps):
        off = j - taps // 2
        src = t + off
        inside = (src >= 0) & (src < n_rows) & ((src < n_ctx) == (t < n_ctx))
        term = jnp.where(inside, ext[HALO + off:HALO + off + tm], 0.0) * w_ref[j:j + 1, :]
        acc = term if acc is None else acc + term
    y = acc * jax.nn.sigmoid(acc)

    def l2(x):
        return x * lax.rsqrt(jnp.sum(x * x, axis=-1, keepdims=True) + L2_EPS)

    nq = q_ref.shape[-1]
    for h in range(nq // GDN_HEAD_K):
        lo = h * GDN_HEAD_K
        q_ref[:, lo:lo + GDN_HEAD_K] = (l2(y[:, lo:lo + GDN_HEAD_K]) * q_scale).astype(q_ref.dtype)
        k_ref[:, lo:lo + GDN_HEAD_K] = l2(y[:, nq + lo:nq + lo + GDN_HEAD_K]).astype(k_ref.dtype)
    v_ref[...] = y[:, 2 * nq:].astype(v_ref.dtype)


def _gdn_prep(u, conv_w, *, n_ctx, block_rows):
    B, T, _ = u.shape
    ch = conv_w.shape[1]
    nq = GDN_QK_DIM
    tm = _largest_divisor(T, (block_rows, 64, 32, 16))
    assert tm % HALO == 0 and conv_w.shape[0] // 2 <= HALO and ch == 2 * nq + GDN_V_DIM
    per_tile = tm // HALO
    last = T // HALO - 1
    return pl.pallas_call(
        functools.partial(_gdn_prep_kernel, n_ctx=n_ctx, n_rows=T, q_scale=GDN_HEAD_K ** -0.5),
        grid=(B, T // tm),
        in_specs=[
            pl.BlockSpec((None, HALO, ch), lambda b, i: (b, jnp.maximum(i * per_tile - 1, 0), 0)),
            pl.BlockSpec((None, tm, ch), lambda b, i: (b, i, 0)),
            pl.BlockSpec((None, HALO, ch), lambda b, i: (b, jnp.minimum((i + 1) * per_tile, last), 0)),
            pl.BlockSpec(conv_w.shape, lambda b, i: (0, 0)),
        ],
        out_specs=[
            pl.BlockSpec((None, tm, nq), lambda b, i: (b, i, 0)),
            pl.BlockSpec((None, tm, nq), lambda b, i: (b, i, 0)),
            pl.BlockSpec((None, tm, GDN_V_DIM), lambda b, i: (b, i, 0)),
        ],
        out_shape=[jax.ShapeDtypeStruct((B, T, nq), BF16), jax.ShapeDtypeStruct((B, T, nq), BF16),
                   jax.ShapeDtypeStruct((B, T, GDN_V_DIM), BF16)],
        compiler_params=pltpu.CompilerParams(
            dimension_semantics=("parallel", "parallel"),
            vmem_limit_bytes=VMEM_LIMIT_BYTES),
        name="gdn_prep",
    )(u, u, u, conv_w)


def _rms_norm(x, g, eps=NORM_EPS):
    return x * lax.rsqrt(jnp.mean(x * x, -1, keepdims=True) + eps) * g


def _l2_normalize(x, eps=L2_EPS):
    return x * lax.rsqrt(jnp.sum(x * x, -1, keepdims=True) + eps)


def _heads(t, dim):
    return t.reshape(t.shape[:-1] + (t.shape[-1] // dim, dim))


def _flat(t):
    return t.reshape(t.shape[:-2] + (t.shape[-2] * t.shape[-1],))


def _segment_shifted(h, offset, n_ctx):
    if offset == 0:
        return h
    T = h.shape[1]
    if offset > 0:
        moved = jnp.pad(h[:, offset:], ((0, 0), (0, offset), (0, 0)))
    else:
        moved = jnp.pad(h[:, :offset], ((0, 0), (-offset, 0), (0, 0)))
    t = jnp.arange(T)
    same_segment = ((t + offset) < n_ctx) == (t < n_ctx)
    return jnp.where(same_segment[None, :, None], moved, 0.0)


def _rwkv_mixer(h, n_ctx, mu, w_rkv, w_o, w0, w1, w2, a0, a1, a2, g1, g2, vecs):
    k_k, k_a, r_k, ln_w, ln_b = (vecs[j] for j in range(5))
    xx = 0.5 * (_segment_shifted(h, -1, n_ctx) + _segment_shifted(h, 1, n_ctx)) - h
    xr, xw, xk, xv, xa, xg = ((h + xx * mu[j]).astype(BF16) for j in range(6))
    r = jnp.dot(xr, w_rkv[0].astype(BF16), preferred_element_type=F32)
    k = jnp.dot(xk, w_rkv[1].astype(BF16), preferred_element_type=F32)
    v = jnp.dot(xv, w_rkv[2].astype(BF16), preferred_element_type=F32)
    gate = jax.nn.sigmoid(jnp.dot(xg, g1.astype(BF16), preferred_element_type=F32)) @ g2
    kk = _flat(_l2_normalize(_heads(k * k_k, RW_HEAD)))
    w_lora = jnp.tanh(jnp.einsum('btd,kde->kbte', xw, w1.astype(BF16), preferred_element_type=F32))
    w_log = -jax.nn.softplus(-(w0[:, None, None, :] + jnp.einsum('kbte,ked->kbtd', w_lora, w2))) - 0.5
    lw = -jnp.exp(w_log)
    a_lora = jnp.einsum('btd,kde->kbte', xa, a1.astype(BF16), preferred_element_type=F32)
    a = jax.nn.sigmoid(a0[:, None, None, :] + jnp.einsum('kbte,ked->kbtd', a_lora, a2))
    kd = k[None] * (1.0 + (a - 1.0) * k_a)
    bb = kk[None] * a
    o = _dplr_scan((r.astype(BF16)[None], lw, kd.astype(BF16), v.astype(BF16)[None],
                    (-kk).astype(BF16)[None], bb.astype(BF16)),
                   n_ctx=n_ctx, groups_per_step=r.shape[-1] // LANES,
                   heads_per_group=LANES // RW_HEAD, delta_rule=False)
    o_sum = _heads(o[0] + o[1], RW_HEAD)
    bonus = jnp.sum(_heads(r * (kd[0] + kd[1]) * r_k, RW_HEAD), -1, keepdims=True) * _heads(v, RW_HEAD)
    mean = jnp.mean(o_sum, -1, keepdims=True)
    var = jnp.mean(jnp.square(o_sum - mean), -1, keepdims=True)
    y = (o_sum - mean) * lax.rsqrt(var + RW_GN_EPS)
    y = _flat(y * _heads(ln_w, RW_HEAD) + _heads(ln_b, RW_HEAD) + bonus)
    return jnp.dot((y * gate).astype(BF16), w_o.astype(BF16), preferred_element_type=F32)


def _axial_rope(t, n_ctx):
    T = t.shape[1]
    pos = jnp.maximum(jnp.arange(T) - n_ctx, 0)
    is_lat = (jnp.arange(T) >= n_ctx).astype(F32)
    row_id = (pos // GRID_W).astype(F32) * is_lat
    col_id = (pos % GRID_W).astype(F32) * is_lat
    nf = MLA_ROPE // 4
    inv = ROPE_THETA ** (-jnp.arange(nf, dtype=F32) / nf)

    def rot(seg, p):
        ang = p[:, None] * inv
        cos = jnp.cos(ang)[:, None, :]
        sin = jnp.sin(ang)[:, None, :]
        s1, s2 = seg[..., :nf], seg[..., nf:]
        return jnp.concatenate([s1 * cos - s2 * sin, s2 * cos + s1 * sin], -1)

    half = MLA_ROPE // 2
    return jnp.concatenate([rot(t[..., :half], row_id), rot(t[..., half:], col_id)], -1)


def _mla_mixer(h, n_ctx, w_in, qa_g, kva_g, w_qb, w_kvb, qk_g, w_o):
    B, T, _ = h.shape
    u = jnp.dot(h.astype(BF16), w_in.astype(BF16), preferred_element_type=F32)
    cq = _rms_norm(u[..., :MLA_Q_LORA], qa_g)
    ckv = _rms_norm(u[..., MLA_Q_LORA:MLA_Q_LORA + MLA_KV_LORA], kva_g)
    k_pe = u[..., MLA_Q_LORA + MLA_KV_LORA:]
    q = (cq @ w_qb).reshape(B, T, MLA_HEADS, MLA_QK)
    kv = (ckv @ w_kvb).reshape(B, T, MLA_HEADS, MLA_NOPE + MLA_V)
    k_nope, v = kv[..., :MLA_NOPE], kv[..., MLA_NOPE:]
    k = jnp.concatenate([k_nope, jnp.broadcast_to(k_pe[:, :, None, :], (B, T, MLA_HEADS, MLA_ROPE))], -1)
    q = _rms_norm(q, qk_g[0])
    k = _rms_norm(k, qk_g[1])
    q = jnp.concatenate([q[..., :MLA_NOPE], _axial_rope(q[..., MLA_NOPE:], n_ctx)], -1)
    k = jnp.concatenate([k[..., :MLA_NOPE], _axial_rope(k[..., MLA_NOPE:], n_ctx)], -1)

    def hm(t):
        return jnp.swapaxes(t, 1, 2).astype(BF16)

    scale = MLA_QK ** -0.5
    qh, kh, vh = hm(q), hm(k), hm(v)
    o_lat = _attention(qh[:, :, n_ctx:], kh, vh, scale=scale, block_q=512)
    o_ctx = _attention(qh[:, :, :n_ctx], kh[:, :, :n_ctx], vh[:, :, :n_ctx], scale=scale, block_q=512)
    o = jnp.concatenate([o_ctx, o_lat], axis=2)
    o = jnp.swapaxes(o, 1, 2).reshape(B, T, MLA_HEADS * MLA_V)
    return jnp.dot(o.astype(BF16), w_o.astype(BF16), preferred_element_type=F32)


def _gdn_mixer(h, n_ctx, w_in, conv_w, a_log, dt_bias, norm_g, w_out):
    qk_end = 2 * GDN_QK_DIM
    v_end = qk_end + GDN_V_DIM
    z_end = v_end + GDN_V_DIM
    b_end = z_end + 2 * GDN_V_HEADS
    u = jnp.dot(h.astype(BF16), w_in.astype(BF16), preferred_element_type=F32)
    q, k, v = _gdn_prep(u, conv_w, n_ctx=n_ctx, block_rows=128)
    z = u[..., v_end:z_end]
    beta = jax.nn.sigmoid(_heads(u[..., z_end:b_end], GDN_V_HEADS))
    g = -jnp.exp(a_log) * jax.nn.softplus(_heads(u[..., b_end:], GDN_V_HEADS) + dt_bias)

    def per_head_lanes(t):
        return jnp.pad(jnp.moveaxis(t, 2, 0), ((0, 0), (0, 0), (0, 0), (0, LANES - GDN_V_HEADS)))

    o = _dplr_scan((q[None], per_head_lanes(g), k[None], v[None], per_head_lanes(beta)),
                   n_ctx=n_ctx, groups_per_step=GDN_V_HEADS, heads_per_group=1, delta_rule=True)
    o = _heads(o[0] + o[1], GDN_HEAD_V)
    y = _flat(_rms_norm(o, norm_g) * jax.nn.silu(_heads(z, GDN_HEAD_V)))
    return jnp.dot(y.astype(BF16), w_out.astype(BF16), preferred_element_type=F32)


def _ec_moe_add(xa, h, n_ctx, with_ctx, router, w13, w2, layer, mod_ctx, mod_lat):
    B, T, D = h.shape
    E = router.shape[-1]
    aff = jnp.swapaxes(jax.nn.softmax(h @ router, axis=-1), 1, 2)
    n_lat = T - n_ctx
    gate, idx = lax.top_k(aff[:, :, n_ctx:], EC_FACTOR * n_lat // E)
    idx = idx + n_ctx
    ctx_slots = 0
    if with_ctx:
        ctx_slots = EC_FACTOR * n_ctx // E
        gate_c, idx_c = lax.top_k(aff[:, :, :n_ctx], ctx_slots)
        gate = jnp.concatenate([gate_c, gate], axis=-1)
        idx = jnp.concatenate([idx_c, idx], axis=-1)
    slots = idx.shape[-1]
    flat_idx = jnp.swapaxes(idx + (jnp.arange(B) * T)[:, None, None], 0, 1).reshape(E, B * slots)
    gate = jnp.swapaxes(gate, 0, 1).reshape(E, B * slots, 1)
    hf = h.astype(BF16).reshape(B * T, D)
    y = _moe_ffn(hf[flat_idx], w13, w2, layer, gate, mod_ctx, mod_lat, ctx_slots=ctx_slots,
                 block_f=256)
    out = xa.reshape(B * T, D).at[flat_idx.reshape(-1)].add(y.reshape(-1, D))
    return out.reshape(B, T, D)


def kernel(x, c, ctx, c_ctx, mod_w, mod_b, norm_g, rw_mu, rw_w_rkv, rw_w_o, rw_w0, rw_w1, rw_w2, rw_a0, rw_a1, rw_a2, rw_g1, rw_g2, rw_vecs, mla_w_in, mla_qa_g, mla_kva_g, mla_w_qb, mla_w_kvb, mla_qk_g, mla_w_o, gdn_w_in, gdn_conv, gdn_a_log, gdn_dt_bias, gdn_norm_g, gdn_w_out, moe_router, moe_w13, moe_w2):
    depth = mod_w.shape[0]
    n_ctx = ctx.shape[1]
    xa = jnp.concatenate([ctx, x], axis=1)
    is_ctx = (jnp.arange(xa.shape[1]) < n_ctx)[None, :, None]
    for i in range(depth):
        last = i == depth - 1
        m_lat = jax.nn.silu(c) @ mod_w[i] + mod_b[i]
        m_ctx = jax.nn.silu(c_ctx)[None] @ mod_w[i] + mod_b[i]
        sl = jnp.split(m_lat[:, None, :], 6, axis=-1)
        sc = jnp.split(m_ctx[:, None, :], 6, axis=-1)
        mod = [jnp.where(is_ctx, sc[j], sl[j]) for j in range(5)]
        h = _rms_norm(xa, norm_g[i, 0]) * (1.0 + mod[1]) + mod[0]
        kind = i % N_MIXERS
        j = i // N_MIXERS
        if kind == 0:
            y = _rwkv_mixer(h, n_ctx, rw_mu[j], rw_w_rkv[j], rw_w_o[j], rw_w0[j], rw_w1[j], rw_w2[j],
                            rw_a0[j], rw_a1[j], rw_a2[j], rw_g1[j], rw_g2[j], rw_vecs[j])
        elif kind == 1:
            y = _mla_mixer(h, n_ctx, mla_w_in[j], mla_qa_g[j], mla_kva_g[j], mla_w_qb[j],
                           mla_w_kvb[j], mla_qk_g[j], mla_w_o[j])
        else:
            y = _gdn_mixer(h, n_ctx, gdn_w_in[j], gdn_conv[j], gdn_a_log[j], gdn_dt_bias[j],
                           gdn_norm_g[j], gdn_w_out[j])
        xa = xa + mod[2] * y
        h2 = _rms_norm(xa, norm_g[i, 1]) * (1.0 + mod[4]) + mod[3]
        xa = _ec_moe_add(xa, h2, n_ctx, not last, moe_router[i], moe_w13, moe_w2, i, sc[5][0], sl[5])
    return xa[:, n_ctx:]
```

```python
import functools

import jax
import jax.numpy as jnp
from jax import lax
from jax.experimental import pallas as pl
from jax.experimental.pallas import tpu as pltpu

F32 = jnp.float32
BF16 = jnp.bfloat16

GRID_W = 64
N_MIXERS = 3
NORM_EPS = 1e-6
L2_EPS = 1e-6
RW_HEAD = 64
RW_GN_EPS = 64e-5
MLA_HEADS = 8
MLA_Q_LORA = 384
MLA_KV_LORA = 256
MLA_NOPE = 128
MLA_ROPE = 64
MLA_V = 128
MLA_QK = MLA_NOPE + MLA_ROPE
ROPE_THETA = 10000.0
GDN_K_HEADS = 8
GDN_V_HEADS = 16
GDN_HEAD_K = 128
GDN_HEAD_V = 128
GDN_CONV = 5
GDN_QK_DIM = GDN_K_HEADS * GDN_HEAD_K
GDN_V_DIM = GDN_V_HEADS * GDN_HEAD_V
EC_FACTOR = 2

LANES = 128
SCAN_CHUNK = 64
VMEM_LIMIT_BYTES = 48 * 1024 * 1024


def _bdot(a, b, dims):
    return lax.dot_general(a.astype(BF16), b.astype(BF16), (dims, ((), ())),
                           preferred_element_type=F32)


def _dot_nn(a, b):
    return _bdot(a, b, ((1,), (0,)))


def _dot_nt(a, b):
    return _bdot(a, b, ((1,), (1,)))


def _dot_tn(a, b):
    return _bdot(a, b, ((0,), (0,)))


def _split3(x):
    hi = x.astype(BF16)
    r1 = x - hi.astype(F32)
    mid = r1.astype(BF16)
    lo = (r1 - mid.astype(F32)).astype(BF16)
    return hi, mid, lo


def _exact3(dot, x, fixed, x_first):
    hi, mid, lo = _split3(x)
    if x_first:
        return (dot(hi, fixed) + dot(mid, fixed)) + dot(lo, fixed)
    return (dot(fixed, hi) + dot(fixed, mid)) + dot(fixed, lo)


def _dplr_kernel(*refs, groups, hpl, reps, delta_rule):
    if delta_rule:
        q_ref, g_ref, k_ref, v_ref, beta_ref, o_ref, state_ref = refs
    else:
        q_ref, lw_ref, k_ref, v_ref, a_ref, b_ref, o_ref, state_ref = refs
    d = pl.program_id(2)
    c = pl.program_id(3)
    C = o_ref.shape[-2]
    hC = hpl * C
    G = range(groups)

    @pl.when(c == 0)
    def _():
        state_ref[...] = jnp.zeros_like(state_ref)

    assert C & (C - 1) == 0 and hpl & (hpl - 1) == 0
    log_c = C.bit_length() - 1
    head_w = LANES // hpl
    log_hw = head_w.bit_length() - 1

    def time_masks(n):
        row = lax.broadcasted_iota(jnp.int32, (n, n), 0)
        col = lax.broadcasted_iota(jnp.int32, (n, n), 1)
        lag = ((row & (C - 1)) - (col & (C - 1))) * (1 - 2 * d)
        same = (row >> log_c) == (col >> log_c)
        return same & (lag >= 0), same & (lag > 0), row == col

    incl_c, _, diag_c = time_masks(C)
    incl, strict, diag = time_masks(hC)
    eye = jnp.where(diag, 1.0, 0.0).astype(F32)
    tri = jnp.where(incl_c, 1.0, 0.0).astype(F32)
    lane = lax.broadcasted_iota(jnp.int32, (1, LANES), 1)
    lane_masks = [(lane >> log_hw) == j for j in range(hpl)]

    def stack(x):
        if hpl == 1:
            return x
        return jnp.concatenate([jnp.where(m, x, jnp.zeros_like(x)) for m in lane_masks], axis=0)

    def fold(x):
        out = x[0:C]
        for j in range(1, hpl):
            out = out + x[j * C:(j + 1) * C]
        return out

    def grp(ref, g, rep):
        j = g // rep
        return ref[:, j * LANES:(j + 1) * LANES]

    if delta_rule:
        assert hpl == 1
        rq, rk, rv = reps
        q = [grp(q_ref, g, rq) for g in G]
        k = [grp(k_ref, g, rk) for g in G]
        vin = [grp(v_ref, g, rv) for g in G]
        g_all = g_ref[...]
        beta_all = beta_ref[...]
        cs_all = _exact3(_dot_nn, g_all, tri, x_first=False)
        cs_rows = _exact3(_dot_tn, cs_all, jnp.where(diag_c, 1.0, 0.0).astype(F32), x_first=True)
        tot_all = jnp.sum(g_all, axis=0, keepdims=True)
        ex_all = cs_all - g_all
        e_ex_all = jnp.exp(ex_all)
        e_cs_all = jnp.exp(cs_all)
        e_end_all = jnp.exp(tot_all - cs_all)
        e_tot_all = jnp.exp(tot_all)
        nb_all = -(beta_all * jnp.exp(g_all))

        def col(x, g, n):
            return jnp.broadcast_to(x[:, g:g + 1], (x.shape[0], n))

        a = k
        b = [col(nb_all, g, LANES) * k[g] for g in G]
        v = [col(beta_all, g, LANES) * vin[g] for g in G]
        at = [a[g] * col(e_ex_all, g, LANES) for g in G]
        qt = [q[g] * col(e_cs_all, g, LANES) for g in G]
        w_end = [col(e_end_all, g, LANES) for g in G]
        w_tot = [col(e_tot_all, g, LANES) for g in G]
        v_st = v
        cs_src = [jnp.broadcast_to(cs_rows[g:g + 1, :], (C, C)) for g in G]
        d_strict = [jnp.exp(jnp.where(strict, col(ex_all, g, C) - cs_src[g], -1e30)) for g in G]
        d_incl = [jnp.exp(jnp.where(incl, col(cs_all, g, C) - cs_src[g], -1e30)) for g in G]
        lhs = [jnp.concatenate([a[g], q[g]], axis=0) for g in G]
        xb = [_dot_nt(lhs[g], b[g]) for g in G]
        xk = [_dot_nt(lhs[g], k[g]) for g in G]
        a_ab = [xb[g][:hC] * d_strict[g] for g in G]
        a_qb = [xb[g][hC:] * d_incl[g] for g in G]
        a_ak = [xk[g][:hC] * d_strict[g] for g in G]
        a_qk = [xk[g][hC:] * d_incl[g] for g in G]
    else:
        rq, rlw, rk, rv, ra, rb = reps
        q = [grp(q_ref, g, rq) for g in G]
        lw = [grp(lw_ref, g, rlw) for g in G]
        k = [grp(k_ref, g, rk) for g in G]
        v = [grp(v_ref, g, rv) for g in G]
        a = [grp(a_ref, g, ra) for g in G]
        b = [grp(b_ref, g, rb) for g in G]
        cs = [_exact3(_dot_nn, lw[g], tri, x_first=False) for g in G]
        total = [jnp.sum(lw[g], axis=0, keepdims=True) for g in G]
        w_end = [jnp.exp(total[g] - cs[g]) for g in G]
        w_tot = [jnp.exp(total[g]) for g in G]
        at = [a[g] * jnp.exp(cs[g] - lw[g]) for g in G]
        qt = [q[g] * jnp.exp(cs[g]) for g in G]
        v_st = [stack(v[g]) for g in G]
        w_inv = [jnp.exp(-cs[g]) for g in G]
        lhs = [jnp.concatenate([stack(at[g]), stack(qt[g])], axis=0) for g in G]
        xb = [_dot_nt(lhs[g], stack(b[g] * w_inv[g])) for g in G]
        xk = [_dot_nt(lhs[g], stack(k[g] * w_inv[g])) for g in G]
        a_ab = [jnp.where(strict, xb[g][:hC], 0.0) for g in G]
        a_qb = [jnp.where(incl, xb[g][hC:], 0.0) for g in G]
        a_ak = [jnp.where(strict, xk[g][:hC], 0.0) for g in G]
        a_qk = [jnp.where(incl, xk[g][hC:], 0.0) for g in G]

    s0 = [state_ref[g] for g in G]
    sp = [_dot_nt(jnp.concatenate([at[g], qt[g]], axis=0), s0[g]) for g in G]

    inv = [eye + a_ab[g] for g in G]
    apow = a_ab
    steps = 1
    while steps * 2 < C:
        apow = [_dot_nn(apow[g], apow[g]) for g in G]
        inv = [inv[g] + _dot_nn(inv[g], apow[g]) for g in G]
        steps *= 2

    rhs = [stack(sp[g][:C]) + _dot_nn(a_ak[g], v_st[g]) for g in G]
    u_st = [_dot_nn(inv[g], rhs[g]) for g in G]
    o = [sp[g][C:] + fold(_dot_nn(a_qb[g], u_st[g]) + _dot_nn(a_qk[g], v_st[g])) for g in G]
    for g in G:
        o_ref[:, g * LANES:(g + 1) * LANES] = o[g].astype(o_ref.dtype)

    upd = [_dot_tn(jnp.concatenate([fold(u_st[g]), v[g].astype(F32)], axis=0),
                   jnp.concatenate([b[g] * w_end[g], k[g] * w_end[g]], axis=0)) for g in G]
    if hpl > 1:
        r2 = lax.broadcasted_iota(jnp.int32, (LANES, LANES), 0) >> log_hw
        c2 = lax.broadcasted_iota(jnp.int32, (LANES, LANES), 1) >> log_hw
        upd = [jnp.where(r2 == c2, upd[g], 0.0) for g in G]
    for g in G:
        state_ref[g] = s0[g] * w_tot[g] + upd[g]


def _dplr_scan(ops, *, n_ctx, groups_per_step, heads_per_group, delta_rule):
    v = ops[3]
    B, T, DV = v.shape[1], v.shape[2], v.shape[3]
    C = SCAN_CHUNK
    G = groups_per_step
    assert T % C == 0 and n_ctx % C == 0 and DV % (G * LANES) == 0
    nc = T // C
    nc_ctx = n_ctx // C
    per_head = (1, 4) if delta_rule else ()
    if delta_rule:
        assert DV == G * LANES

    def chunk_index(d, c):
        rev = jnp.where(c < nc_ctx, nc_ctx - 1 - c, nc + nc_ctx - 1 - c)
        return jnp.where(d == 0, c, rev)

    reps, specs = [], []
    for i, x in enumerate(ops):
        per_dir = x.shape[0] == 2
        if i in per_head:
            assert x.shape[3] == LANES and G <= LANES
            specs.append(pl.BlockSpec(
                (None, None, C, LANES),
                functools.partial(
                    lambda bi, hg, d, c, per_dir: (d if per_dir else 0, bi, chunk_index(d, c), 0),
                    per_dir=per_dir)))
            continue
        assert DV % x.shape[3] == 0
        rep = DV // x.shape[3]
        assert G % rep == 0
        reps.append(rep)
        specs.append(pl.BlockSpec(
            (None, None, C, (G // rep) * LANES),
            functools.partial(
                lambda bi, hg, d, c, per_dir: (d if per_dir else 0, bi, chunk_index(d, c), hg),
                per_dir=per_dir)))
    out_spec = pl.BlockSpec((None, None, C, G * LANES),
                            lambda bi, hg, d, c: (d, bi, chunk_index(d, c), hg))
    return pl.pallas_call(
        functools.partial(_dplr_kernel, groups=G, hpl=heads_per_group, reps=tuple(reps),
                          delta_rule=delta_rule),
        grid=(B, DV // (G * LANES), 2, nc),
        in_specs=specs,
        out_specs=out_spec,
        out_shape=jax.ShapeDtypeStruct((2, B, T, DV), F32),
        scratch_shapes=[pltpu.VMEM((G, LANES, LANES), F32)],
        compiler_params=pltpu.CompilerParams(
            dimension_semantics=("parallel", "parallel", "arbitrary", "arbitrary"),
            vmem_limit_bytes=VMEM_LIMIT_BYTES),
        name="dplr_scan",
    )(*ops)


def _attention_kernel(q_ref, k_ref, v_ref, o_ref, *, scale, tk):
    q = q_ref[...]
    dv = o_ref.shape[-1]
    c = scale * 1.4426950408889634
    n_chunks = k_ref.shape[0] // tk

    def scores(j):
        return _dot_nt(q, k_ref[j * tk:(j + 1) * tk, :])

    s = scores(0)
    m = acc = None
    for j in range(n_chunks):
        s_next = scores(j + 1) if j + 1 < n_chunks else None
        m_chunk = jnp.max(s, axis=-1, keepdims=True)
        m_new = m_chunk if m is None else jnp.maximum(m, m_chunk)
        p = jnp.exp2((s - m_new) * c)
        pv = _dot_nn(p, v_ref[j * tk:(j + 1) * tk, :])
        acc = pv if acc is None else jnp.exp2((m - m_new) * c) * acc + pv
        m, s = m_new, s_next
    o_ref[...] = (acc[:, :dv] / acc[:, dv:dv + 1]).astype(o_ref.dtype)


def _largest_divisor(n, candidates):
    return next(c for c in candidates if n % c == 0)


def _attention(q, k, v, *, scale, block_q):
    B, H, Tq, dk = q.shape
    Tk, dv = k.shape[2], v.shape[3]
    tq = min(block_q, Tq)
    assert Tq % tq == 0 and dv % LANES == 0
    tk = _largest_divisor(Tk, (1408, 768, 512, 256, 128, 64, 32, 16))
    v = jnp.concatenate([v, jnp.ones((B, H, Tk, LANES), v.dtype)], axis=-1)
    return pl.pallas_call(
        functools.partial(_attention_kernel, scale=scale, tk=tk),
        grid=(B, H, Tq // tq),
        in_specs=[
            pl.BlockSpec((None, None, tq, dk), lambda b, h, i: (b, h, i, 0)),
            pl.BlockSpec((None, None, Tk, dk), lambda b, h, i: (b, h, 0, 0)),
            pl.BlockSpec((None, None, Tk, dv + LANES), lambda b, h, i: (b, h, 0, 0)),
        ],
        out_specs=pl.BlockSpec((None, None, tq, dv), lambda b, h, i: (b, h, i, 0)),
        out_shape=jax.ShapeDtypeStruct((B, H, Tq, dv), F32),
        compiler_params=pltpu.CompilerParams(
            dimension_semantics=("parallel", "parallel", "arbitrary"),
            vmem_limit_bytes=VMEM_LIMIT_BYTES),
        name="mla_attention",
    )(q, k, v)


def _moe_ffn_kernel(x_ref, wg_ref, wu_ref, wd_ref, gate_ref, mod_ctx_ref, mod_lat_ref, o_ref, *,
                    ctx_slots):
    f = pl.program_id(2)

    @pl.when(f == 0)
    def _():
        o_ref[...] = jnp.zeros_like(o_ref)

    x = x_ref[...]
    g = _dot_nn(x, wg_ref[...])
    u = _dot_nn(x, wu_ref[...])
    act = (g * jax.nn.sigmoid(g)) * u
    o_ref[...] += _dot_nn(act, wd_ref[...])

    @pl.when(f == pl.num_programs(2) - 1)
    def _():
        rows = lax.broadcasted_iota(jnp.int32, (o_ref.shape[0], 1), 0)
        mod = jnp.where(rows < ctx_slots, mod_ctx_ref[...], mod_lat_ref[...])
        o_ref[...] = o_ref[...] * gate_ref[...] * mod


def _moe_ffn(x, w13, w2, layer, gate, mod_ctx, mod_lat, *, ctx_slots, block_f):
    E, M, D = x.shape
    B = mod_lat.shape[0]
    Fh = w2.shape[2]
    tm = M // B
    tf = min(block_f, Fh)
    assert M % B == 0 and Fh % tf == 0
    nf = Fh // tf
    return pl.pallas_call(
        functools.partial(_moe_ffn_kernel, ctx_slots=ctx_slots),
        grid=(E, B, nf),
        in_specs=[
            pl.BlockSpec((None, tm, D), lambda e, m, f: (e, m, 0)),
            pl.BlockSpec((None, None, D, tf), lambda e, m, f: (layer, e, 0, f)),
            pl.BlockSpec((None, None, D, tf), lambda e, m, f: (layer, e, 0, nf + f)),
            pl.BlockSpec((None, None, tf, D), lambda e, m, f: (layer, e, f, 0)),
            pl.BlockSpec((None, tm, 1), lambda e, m, f: (e, m, 0)),
            pl.BlockSpec((1, D), lambda e, m, f: (0, 0)),
            pl.BlockSpec((None, 1, D), lambda e, m, f: (m, 0, 0)),
        ],
        out_specs=pl.BlockSpec((None, tm, D), lambda e, m, f: (e, m, 0)),
        out_shape=jax.ShapeDtypeStruct((E, M, D), F32),
        compiler_params=pltpu.CompilerParams(
            dimension_semantics=("parallel", "parallel", "arbitrary"),
            vmem_limit_bytes=VMEM_LIMIT_BYTES),
        name="moe_ffn",
    )(x, w13, w13, w2, gate, mod_ctx, mod_lat)


HALO = 8


def _gdn_prep_kernel(prev_ref, cur_ref, next_ref, w_ref, q_ref, k_ref, v_ref, *, n_ctx, n_rows,
                     q_scale):
    tm = cur_ref.shape[0]
    taps = w_ref.shape[0]
    ext = jnp.concatenate([prev_ref[...], cur_ref[...], next_ref[...]], axis=0)
    t = pl.program_id(1) * tm + lax.broadcasted_iota(jnp.int32, (tm, 1), 0)
    acc = None
    for j in range(taps):
        off = j - taps // 2
        src = t + off
        inside = (src >= 0) & (src < n_rows) & ((src < n_ctx) == (t < n_ctx))
        term = jnp.where(inside, ext[HALO + off:HALO + off + tm], 0.0) * w_ref[j:j + 1, :]
        acc = term if acc is None else acc + term
    y = acc * jax.nn.sigmoid(acc)

    def l2(x):
        return x * lax.rsqrt(jnp.sum(x * x, axis=-1, keepdims=True) + L2_EPS)

    nq = q_ref.shape[-1]
    for h in range(nq // GDN_HEAD_K):
        lo = h * GDN_HEAD_K
        q_ref[:, lo:lo + GDN_HEAD_K] = (l2(y[:, lo:lo + GDN_HEAD_K]) * q_scale).astype(q_ref.dtype)
        k_ref[:, lo:lo + GDN_HEAD_K] = l2(y[:, nq + lo:nq + lo + GDN_HEAD_K]).astype(k_ref.dtype)
    v_ref[...] = y[:, 2 * nq:].astype(v_ref.dtype)


def _gdn_prep(u, conv_w, *, n_ctx, block_rows):
    B, T, _ = u.shape
    ch = conv_w.shape[1]
    nq = GDN_QK_DIM
    tm = _largest_divisor(T, (block_rows, 64, 32, 16))
    assert tm % HALO == 0 and conv_w.shape[0] // 2 <= HALO and ch == 2 * nq + GDN_V_DIM
    per_tile = tm // HALO
    last = T // HALO - 1
    return pl.pallas_call(
        functools.partial(_gdn_prep_kernel, n_ctx=n_ctx, n_rows=T, q_scale=GDN_HEAD_K ** -0.5),
        grid=(B, T // tm),
        in_specs=[
            pl.BlockSpec((None, HALO, ch), lambda b, i: (b, jnp.maximum(i * per_tile - 1, 0), 0)),
            pl.BlockSpec((None, tm, ch), lambda b, i: (b, i, 0)),
            pl.BlockSpec((None, HALO, ch), lambda b, i: (b, jnp.minimum((i + 1) * per_tile, last), 0)),
            pl.BlockSpec(conv_w.shape, lambda b, i: (0, 0)),
        ],
        out_specs=[
            pl.BlockSpec((None, tm, nq), lambda b, i: (b, i, 0)),
            pl.BlockSpec((None, tm, nq), lambda b, i: (b, i, 0)),
            pl.BlockSpec((None, tm, GDN_V_DIM), lambda b, i: (b, i, 0)),
        ],
        out_shape=[jax.ShapeDtypeStruct((B, T, nq), BF16), jax.ShapeDtypeStruct((B, T, nq), BF16),
                   jax.ShapeDtypeStruct((B, T, GDN_V_DIM), BF16)],
        compiler_params=pltpu.CompilerParams(
            dimension_semantics=("parallel", "parallel"),
            vmem_limit_bytes=VMEM_LIMIT_BYTES),
        name="gdn_prep",
    )(u, u, u, conv_w)


def _rms_norm(x, g, eps=NORM_EPS):
    return x * lax.rsqrt(jnp.mean(x * x, -1, keepdims=True) + eps) * g


def _l2_normalize(x, eps=L2_EPS):
    return x * lax.rsqrt(jnp.sum(x * x, -1, keepdims=True) + eps)


def _heads(t, dim):
    return t.reshape(t.shape[:-1] + (t.shape[-1] // dim, dim))


def _flat(t):
    return t.reshape(t.shape[:-2] + (t.shape[-2] * t.shape[-1],))


def _segment_shifted(h, offset, n_ctx):
    if offset == 0:
        return h
    T = h.shape[1]
    if offset > 0:
        moved = jnp.pad(h[:, offset:], ((0, 0), (0, offset), (0, 0)))
    else:
        moved = jnp.pad(h[:, :offset], ((0, 0), (-offset, 0), (0, 0)))
    t = jnp.arange(T)
    same_segment = ((t + offset) < n_ctx) == (t < n_ctx)
    return jnp.where(same_segment[None, :, None], moved, 0.0)


def _rwkv_mixer(h, n_ctx, mu, w_rkv, w_o, w0, w1, w2, a0, a1, a2, g1, g2, vecs):
    k_k, k_a, r_k, ln_w, ln_b = (vecs[j] for j in range(5))
    xx = 0.5 * (_segment_shifted(h, -1, n_ctx) + _segment_shifted(h, 1, n_ctx)) - h
    xr, xw, xk, xv, xa, xg = ((h + xx * mu[j]).astype(BF16) for j in range(6))
    r = jnp.dot(xr, w_rkv[0].astype(BF16), preferred_element_type=F32)
    k = jnp.dot(xk, w_rkv[1].astype(BF16), preferred_element_type=F32)
    v = jnp.dot(xv, w_rkv[2].astype(BF16), preferred_element_type=F32)
    gate = jax.nn.sigmoid(jnp.dot(xg, g1.astype(BF16), preferred_element_type=F32)) @ g2
    kk = _flat(_l2_normalize(_heads(k * k_k, RW_HEAD)))
    w_lora = jnp.tanh(jnp.einsum('btd,kde->kbte', xw, w1.astype(BF16), preferred_element_type=F32))
    w_log = -jax.nn.softplus(-(w0[:, None, None, :] + jnp.einsum('kbte,ked->kbtd', w_lora, w2))) - 0.5
    lw = -jnp.exp(w_log)
    a_lora = jnp.einsum('btd,kde->kbte', xa, a1.astype(BF16), preferred_element_type=F32)
    a = jax.nn.sigmoid(a0[:, None, None, :] + jnp.einsum('kbte,ked->kbtd', a_lora, a2))
    kd = k[None] * (1.0 + (a - 1.0) * k_a)
    bb = kk[None] * a
    o = _dplr_scan((r.astype(BF16)[None], lw, kd.astype(BF16), v.astype(BF16)[None],
                    (-kk).astype(BF16)[None], bb.astype(BF16)),
                   n_ctx=n_ctx, groups_per_step=r.shape[-1] // LANES,
                   heads_per_group=LANES // RW_HEAD, delta_rule=False)
    o_sum = _heads(o[0] + o[1], RW_HEAD)
    bonus = jnp.sum(_heads(r * (kd[0] + kd[1]) * r_k, RW_HEAD), -1, keepdims=True) * _heads(v, RW_HEAD)
    mean = jnp.mean(o_sum, -1, keepdims=True)
    var = jnp.mean(jnp.square(o_sum - mean), -1, keepdims=True)
    y = (o_sum - mean) * lax.rsqrt(var + RW_GN_EPS)
    y = _flat(y * _heads(ln_w, RW_HEAD) + _heads(ln_b, RW_HEAD) + bonus)
    return jnp.dot((y * gate).astype(BF16), w_o.astype(BF16), preferred_element_type=F32)


def _axial_rope(t, n_ctx):
    T = t.shape[1]
    pos = jnp.maximum(jnp.arange(T) - n_ctx, 0)
    is_lat = (jnp.arange(T) >= n_ctx).astype(F32)
    row_id = (pos // GRID_W).astype(F32) * is_lat
    col_id = (pos % GRID_W).astype(F32) * is_lat
    nf = MLA_ROPE // 4
    inv = ROPE_THETA ** (-jnp.arange(nf, dtype=F32) / nf)

    def rot(seg, p):
        ang = p[:, None] * inv
        cos = jnp.cos(ang)[:, None, :]
        sin = jnp.sin(ang)[:, None, :]
        s1, s2 = seg[..., :nf], seg[..., nf:]
        return jnp.concatenate([s1 * cos - s2 * sin, s2 * cos + s1 * sin], -1)

    half = MLA_ROPE // 2
    return jnp.concatenate([rot(t[..., :half], row_id), rot(t[..., half:], col_id)], -1)


def _mla_mixer(h, n_ctx, w_in, qa_g, kva_g, w_qb, w_kvb, qk_g, w_o):
    B, T, _ = h.shape
    u = jnp.dot(h.astype(BF16), w_in.astype(BF16), preferred_element_type=F32)
    cq = _rms_norm(u[..., :MLA_Q_LORA], qa_g)
    ckv = _rms_norm(u[..., MLA_Q_LORA:MLA_Q_LORA + MLA_KV_LORA], kva_g)
    k_pe = u[..., MLA_Q_LORA + MLA_KV_LORA:]
    q = (cq @ w_qb).reshape(B, T, MLA_HEADS, MLA_QK)
    kv = (ckv @ w_kvb).reshape(B, T, MLA_HEADS, MLA_NOPE + MLA_V)
    k_nope, v = kv[..., :MLA_NOPE], kv[..., MLA_NOPE:]
    k = jnp.concatenate([k_nope, jnp.broadcast_to(k_pe[:, :, None, :], (B, T, MLA_HEADS, MLA_ROPE))], -1)
    q = _rms_norm(q, qk_g[0])
    k = _rms_norm(k, qk_g[1])
    q = jnp.concatenate([q[..., :MLA_NOPE], _axial_rope(q[..., MLA_NOPE:], n_ctx)], -1)
    k = jnp.concatenate([k[..., :MLA_NOPE], _axial_rope(k[..., MLA_NOPE:], n_ctx)], -1)

    def hm(t):
        return jnp.swapaxes(t, 1, 2).astype(BF16)

    scale = MLA_QK ** -0.5
    qh, kh, vh = hm(q), hm(k), hm(v)
    o_lat = _attention(qh[:, :, n_ctx:], kh, vh, scale=scale, block_q=512)
    o_ctx = _attention(qh[:, :, :n_ctx], kh[:, :, :n_ctx], vh[:, :, :n_ctx], scale=scale, block_q=512)
    o = jnp.concatenate([o_ctx, o_lat], axis=2)
    o = jnp.swapaxes(o, 1, 2).reshape(B, T, MLA_HEADS * MLA_V)
    return jnp.dot(o.astype(BF16), w_o.astype(BF16), preferred_element_type=F32)


def _gdn_mixer(h, n_ctx, w_in, conv_w, a_log, dt_bias, norm_g, w_out):
    qk_end = 2 * GDN_QK_DIM
    v_end = qk_end + GDN_V_DIM
    z_end = v_end + GDN_V_DIM
    b_end = z_end + 2 * GDN_V_HEADS
    hb = h.astype(BF16)
    u_conv = jnp.dot(hb, w_in[:, :v_end].astype(BF16), preferred_element_type=F32)
    u = jnp.dot(hb, w_in[:, v_end:].astype(BF16), preferred_element_type=F32)
    q, k, v = _gdn_prep(u_conv, conv_w, n_ctx=n_ctx, block_rows=128)
    z = u[..., :GDN_V_DIM]
    beta = jax.nn.sigmoid(_heads(u[..., z_end - v_end:b_end - v_end], GDN_V_HEADS))
    g = -jnp.exp(a_log) * jax.nn.softplus(_heads(u[..., b_end - v_end:], GDN_V_HEADS) + dt_bias)

    def per_head_lanes(t):
        return jnp.pad(jnp.moveaxis(t, 2, 0), ((0, 0), (0, 0), (0, 0), (0, LANES - GDN_V_HEADS)))

    o = _dplr_scan((q[None], per_head_lanes(g), k[None], v[None], per_head_lanes(beta)),
                   n_ctx=n_ctx, groups_per_step=GDN_V_HEADS, heads_per_group=1, delta_rule=True)
    o = _heads(o[0] + o[1], GDN_HEAD_V)
    y = _flat(_rms_norm(o, norm_g) * jax.nn.silu(_heads(z, GDN_HEAD_V)))
    return jnp.dot(y.astype(BF16), w_out.astype(BF16), preferred_element_type=F32)


def _ec_moe_add(xa, h, n_ctx, with_ctx, router, w13, w2, layer, mod_ctx, mod_lat):
    B, T, D = h.shape
    E = router.shape[-1]
    aff = jnp.swapaxes(jax.nn.softmax(h @ router, axis=-1), 1, 2)
    n_lat = T - n_ctx
    gate, idx = lax.top_k(aff[:, :, n_ctx:], EC_FACTOR * n_lat // E)
    idx = idx + n_ctx
    ctx_slots = 0
    if with_ctx:
        ctx_slots = EC_FACTOR * n_ctx // E
        gate_c, idx_c = lax.top_k(aff[:, :, :n_ctx], ctx_slots)
        gate = jnp.concatenate([gate_c, gate], axis=-1)
        idx = jnp.concatenate([idx_c, idx], axis=-1)
    slots = idx.shape[-1]
    flat_idx = jnp.swapaxes(idx + (jnp.arange(B) * T)[:, None, None], 0, 1).reshape(E, B * slots)
    gate = jnp.swapaxes(gate, 0, 1).reshape(E, B * slots, 1)
    hf = h.astype(BF16).reshape(B * T, D)
    y = _moe_ffn(hf[flat_idx], w13, w2, layer, gate, mod_ctx, mod_lat, ctx_slots=ctx_slots,
                 block_f=256)
    out = xa.reshape(B * T, D).at[flat_idx.reshape(-1)].add(y.reshape(-1, D))
    return out.reshape(B, T, D)


def kernel(x, c, ctx, c_ctx, mod_w, mod_b, norm_g, rw_mu, rw_w_rkv, rw_w_o, rw_w0, rw_w1, rw_w2, rw_a0, rw_a1, rw_a2, rw_g1, rw_g2, rw_vecs, mla_w_in, mla_qa_g, mla_kva_g, mla_w_qb, mla_w_kvb, mla_qk_g, mla_w_o, gdn_w_in, gdn_conv, gdn_a_log, gdn_dt_bias, gdn_norm_g, gdn_w_out, moe_router, moe_w13, moe_w2):
    depth = mod_w.shape[0]
    n_ctx = ctx.shape[1]
    xa = jnp.concatenate([ctx, x], axis=1)
    is_ctx = (jnp.arange(xa.shape[1]) < n_ctx)[None, :, None]
    for i in range(depth):
        last = i == depth - 1
        m_lat = jax.nn.silu(c) @ mod_w[i] + mod_b[i]
        m_ctx = jax.nn.silu(c_ctx)[None] @ mod_w[i] + mod_b[i]
        sl = jnp.split(m_lat[:, None, :], 6, axis=-1)
        sc = jnp.split(m_ctx[:, None, :], 6, axis=-1)
        mod = [jnp.where(is_ctx, sc[j], sl[j]) for j in range(5)]
        h = _rms_norm(xa, norm_g[i, 0]) * (1.0 + mod[1]) + mod[0]
        kind = i % N_MIXERS
        j = i // N_MIXERS
        if kind == 0:
            y = _rwkv_mixer(h, n_ctx, rw_mu[j], rw_w_rkv[j], rw_w_o[j], rw_w0[j], rw_w1[j], rw_w2[j],
                            rw_a0[j], rw_a1[j], rw_a2[j], rw_g1[j], rw_g2[j], rw_vecs[j])
        elif kind == 1:
            y = _mla_mixer(h, n_ctx, mla_w_in[j], mla_qa_g[j], mla_kva_g[j], mla_w_qb[j],
                           mla_w_kvb[j], mla_qk_g[j], mla_w_o[j])
        else:
            y = _gdn_mixer(h, n_ctx, gdn_w_in[j], gdn_conv[j], gdn_a_log[j], gdn_dt_bias[j],
                           gdn_norm_g[j], gdn_w_out[j])
        xa = xa + mod[2] * y
        h2 = _rms_norm(xa, norm_g[i, 1]) * (1.0 + mod[4]) + mod[3]
        xa = _ec_moe_add(xa, h2, n_ctx, not last, moe_router[i], moe_w13, moe_w2, i, sc[5][0], sl[5])
    return xa[:, n_ctx:]
```
